```python
import jax, jax.numpy as jnp
from jax import lax
import numpy as np

D_MODEL = 1024
BATCH = 2
SEQ = 16384
DEPTH = 1
DEC_BATCH = 8
DEC_SEQ = 16
PAST_LEN = 4096

CHUNK = 64
N_PREV_CHUNKS = 8
ATT_WINDOW = N_PREV_CHUNKS * CHUNK
N_HEADS_A = 8
HEAD_DIM_A = 64
D_A = N_HEADS_A * HEAD_DIM_A
REL_CLIP = 128
N_REL = CHUNK + REL_CLIP
N_HEADS_B = 4
HEAD_K_B = 128
HEAD_V_B = 128
D_B = N_HEADS_B * HEAD_K_B
D_B_V = N_HEADS_B * HEAD_V_B
D_FF = 2816
CONV_W = 3
EPS = 1e-6
IN_SPLITS = (D_A, D_A, D_A, D_B, D_B, D_B_V, D_B_V, D_MODEL, D_MODEL)
D_IN = sum(IN_SPLITS)

kernel_name = "hybrid_chunk_attn_hgrn2_convffn_step"


def rms_norm(x, g):
    xf = x.astype(jnp.float32)
    y = xf * lax.rsqrt(jnp.mean(xf * xf, axis=-1, keepdims=True) + EPS)
    return (y * g.astype(jnp.float32)).astype(x.dtype)


def split_in(h, w_in):
    z = h @ w_in
    offs = [int(o) for o in np.cumsum(IN_SPLITS)[:-1]]
    return jnp.split(z, offs, axis=-1)


def rel_bias_mask(qpos, kpos, table):
    d = qpos[..., :, None] - kpos[..., None, :]
    idx = jnp.clip(d, -(CHUNK - 1), REL_CLIP) + (CHUNK - 1)
    bias = jnp.moveaxis(table[:, idx], 0, -3)
    qc = (qpos // CHUNK)[..., :, None]
    kc = (kpos // CHUNK)[..., None, :]
    valid = (kpos[..., None, :] >= 0) & (kc <= qc) & (kc >= qc - N_PREV_CHUNKS)
    return bias, valid[..., None, :, :]


def band_attention(q, k, v, bias, valid):
    s = jnp.einsum('...hqd,...hkd->...hqk', q, k).astype(jnp.float32) * (HEAD_DIM_A ** -0.5)
    s = jnp.where(valid, s + bias.astype(jnp.float32), -jnp.inf)
    p = jax.nn.softmax(s, axis=-1).astype(v.dtype)
    return jnp.einsum('...hqk,...hkd->...hqd', p, v)


def attn_prompt(qa, ka, va, table):
    B, L, _ = qa.shape
    nC = L // CHUNK
    def to_chunks(t):
        return t.reshape(B, nC, CHUNK, N_HEADS_A, HEAD_DIM_A).transpose(0, 1, 3, 2, 4)
    qc, kc, vc = to_chunks(qa), to_chunks(ka), to_chunks(va)
    pad = ((0, 0), (N_PREV_CHUNKS, 0), (0, 0), (0, 0), (0, 0))
    kp, vp = jnp.pad(kc, pad), jnp.pad(vc, pad)
    kb = jnp.concatenate([kp[:, j:j + nC] for j in range(N_PREV_CHUNKS + 1)], axis=3)
    vb = jnp.concatenate([vp[:, j:j + nC] for j in range(N_PREV_CHUNKS + 1)], axis=3)
    qpos = jnp.arange(L).reshape(nC, CHUNK)
    kpos = (jnp.arange(nC)[:, None] - N_PREV_CHUNKS) * CHUNK + jnp.arange((N_PREV_CHUNKS + 1) * CHUNK)[None, :]
    bias, valid = rel_bias_mask(qpos, kpos, table)
    o = band_attention(qc, kb, vb, bias, valid)
    o = o.transpose(0, 1, 3, 2, 4).reshape(B, L, D_A)
    w = min(ATT_WINDOW, L)
    def last_rows(t):
        return t.reshape(B, L, N_HEADS_A, HEAD_DIM_A)[:, L - w:].transpose(0, 2, 1, 3)
    return o, last_rows(ka), last_rows(va)


def attn_sample(qa, ka, va, cache_k, cache_v, table):
    B, L, _ = qa.shape
    def heads(t):
        return t.reshape(B, L, N_HEADS_A, HEAD_DIM_A).transpose(0, 2, 1, 3)
    q, k, v = heads(qa), heads(ka), heads(va)
    kk = jnp.concatenate([cache_k.astype(k.dtype), k], axis=2)
    vv = jnp.concatenate([cache_v.astype(v.dtype), v], axis=2)
    w = cache_k.shape[2]
    qpos = PAST_LEN + jnp.arange(L)
    kpos = jnp.concatenate([PAST_LEN - w + jnp.arange(w), qpos])
    bias, valid = rel_bias_mask(qpos, kpos, table)
    o = band_attention(q, kk, vv, bias, valid)
    return o.transpose(0, 2, 1, 3).reshape(B, L, D_A), k, v


def hgrn2_prep(qb, fb, ib, lb):
    B, L, _ = qb.shape
    f = lb + (1.0 - lb) * jax.nn.sigmoid(fb.astype(jnp.float32))
    q = jax.nn.silu(qb.astype(jnp.float32))
    k = 1.0 - f
    logf = jnp.log(f)
    hk = lambda t: t.reshape(B, L, N_HEADS_B, HEAD_K_B)
    v = ib.astype(jnp.float32).reshape(B, L, N_HEADS_B, HEAD_V_B)
    return hk(q), hk(k), v, hk(logf)


def hgrn2_scan(q, k, v, logf, S0, block):
    B, L = q.shape[:2]
    n = L // block
    def blocks(t):
        return t.reshape(B, n, block, t.shape[2], t.shape[3]).transpose(1, 0, 3, 2, 4)
    tri = jnp.tril(jnp.ones((block, block), dtype=bool))
    def step(S, inp):
        qc, kc, vc, gc = inp
        b = jnp.cumsum(gc, axis=2)
        inter = jnp.einsum('bhtk,bhkv->bhtv', qc * jnp.exp(b), S)
        diff = b[:, :, :, None, :] - b[:, :, None, :, :]
        decay = jnp.exp(jnp.where(tri[:, :, None], diff, -jnp.inf))
        A = jnp.einsum('bhtk,bhsk,bhtsk->bhts', qc, kc, decay)
        intra = jnp.einsum('bhts,bhsv->bhtv', A, vc)
        bC = b[:, :, -1:, :]
        S_new = jnp.exp(bC[:, :, 0, :])[..., None] * S + jnp.einsum('bhsk,bhsv->bhkv', kc * jnp.exp(bC - b), vc)
        return S_new, inter + intra
    S_fin, o = lax.scan(step, S0.astype(jnp.float32), (blocks(q), blocks(k), blocks(v), blocks(logf)))
    o = o.transpose(1, 0, 3, 2, 4).reshape(B, L, N_HEADS_B, HEAD_V_B)
    return o, S_fin


def hgrn2_out(o, gb, g_norm):
    B, L = o.shape[:2]
    on = o * lax.rsqrt(jnp.mean(o * o, axis=-1, keepdims=True) + EPS) * g_norm.astype(jnp.float32)
    return on.reshape(B, L, D_B_V) * jax.nn.silu(gb.astype(jnp.float32))


def conv_ffn(h, prev, w_gate, w_up, conv_w, conv_b, w_down):
    a = h @ w_gate
    L = a.shape[1]
    ap = jnp.concatenate([prev.astype(a.dtype), a], axis=1)
    ac = conv_b
    for j in range(CONV_W):
        ac = ac + conv_w[j] * ap[:, j:j + L]
    y = (jax.nn.gelu(ac) * (h @ w_up)) @ w_down
    return y, ap[:, ap.shape[1] - (CONV_W - 1):]


def layer(x, lb, cache_k, cache_v, S0, conv_prev, block, norm_mix_g, w_in, rel_bias, hgrn_norm_g,
          w_branch_a, w_branch_b, w_out, norm_ffn_g, w_ffn_gate, w_ffn_up, ffn_conv_w, ffn_conv_b, w_ffn_down):
    h = rms_norm(x, norm_mix_g)
    qa, ka, va, qb, fb, ib, gb, za, zb = split_in(h, w_in)
    if cache_k is None:
        oa, k_rows, v_rows = attn_prompt(qa, ka, va, rel_bias)
    else:
        oa, k_rows, v_rows = attn_sample(qa, ka, va, cache_k, cache_v, rel_bias)
    q, k, v, logf = hgrn2_prep(qb, fb, ib, lb)
    ob, S_new = hgrn2_scan(q, k, v, logf, S0, block)
    ob = hgrn2_out(ob, gb, hgrn_norm_g).astype(x.dtype)
    merged = jax.nn.sigmoid(za) * (oa @ w_branch_a) + jax.nn.sigmoid(zb) * (ob @ w_branch_b)
    x = x + merged @ w_out
    y, conv_new = conv_ffn(rms_norm(x, norm_ffn_g), conv_prev, w_ffn_gate, w_ffn_up, ffn_conv_w, ffn_conv_b, w_ffn_down)
    return x + y, k_rows, v_rows, S_new, conv_new


def setup_inputs(seed: int = 0) -> dict:
    key = jax.random.key(seed)
    ks = jax.random.split(key, 24)
    nrm = lambda i, shape, s=1.0: jax.random.normal(ks[i], shape, jnp.float32) * s
    w_att = min(ATT_WINDOW, PAST_LEN)
    return {
        "x_prompt": nrm(0, (BATCH, SEQ, D_MODEL)),
        "x_sample": nrm(1, (DEC_BATCH, DEC_SEQ, D_MODEL)),
        "cache_attn_k": nrm(2, (DEPTH, DEC_BATCH, N_HEADS_A, w_att, HEAD_DIM_A)),
        "cache_attn_v": nrm(3, (DEPTH, DEC_BATCH, N_HEADS_A, w_att, HEAD_DIM_A)),
        "state_hgrn": nrm(4, (DEPTH, DEC_BATCH, N_HEADS_B, HEAD_K_B, HEAD_V_B)),
        "state_ffn_conv": nrm(5, (DEPTH, DEC_BATCH, CONV_W - 1, D_FF)),
        "norm_mix_g": 1.0 + nrm(6, (DEPTH, D_MODEL), 0.05),
        "w_in": nrm(7, (DEPTH, D_MODEL, D_IN), D_MODEL ** -0.5),
        "rel_bias": nrm(8, (DEPTH, N_HEADS_A, N_REL), 0.5),
        "hgrn_lb_logits": nrm(9, (DEPTH + 1, D_B)),
        "hgrn_norm_g": 1.0 + nrm(10, (DEPTH, HEAD_V_B), 0.05),
        "w_branch_a": nrm(11, (DEPTH, D_A, D_MODEL), D_A ** -0.5),
        "w_branch_b": nrm(12, (DEPTH, D_B_V, D_MODEL), D_B_V ** -0.5),
        "w_out": nrm(13, (DEPTH, D_MODEL, D_MODEL), D_MODEL ** -0.5),
        "norm_ffn_g": 1.0 + nrm(14, (DEPTH, D_MODEL), 0.05),
        "w_ffn_gate": nrm(15, (DEPTH, D_MODEL, D_FF), D_MODEL ** -0.5),
        "w_ffn_up": nrm(16, (DEPTH, D_MODEL, D_FF), D_MODEL ** -0.5),
        "ffn_conv_w": nrm(17, (DEPTH, CONV_W, D_FF), CONV_W ** -0.5),
        "ffn_conv_b": nrm(18, (DEPTH, D_FF), 0.01),
        "w_ffn_down": nrm(19, (DEPTH, D_FF, D_MODEL), D_FF ** -0.5),
        "norm_final_g": 1.0 + nrm(20, (D_MODEL,), 0.05),
    }


def reference(x_prompt, x_sample, cache_attn_k, cache_attn_v, state_hgrn, state_ffn_conv,
              norm_mix_g, w_in, rel_bias, hgrn_lb_logits, hgrn_norm_g, w_branch_a, w_branch_b, w_out,
              norm_ffn_g, w_ffn_gate, w_ffn_up, ffn_conv_w, ffn_conv_b, w_ffn_down, norm_final_g):
    lbs = jnp.cumsum(jax.nn.softmax(hgrn_lb_logits.astype(jnp.float32), axis=0), axis=0)
    xp, xs = x_prompt, x_sample
    Bp, Bs = x_prompt.shape[0], x_sample.shape[0]
    kp_l, vp_l, sp_l, cp_l, ks_l, vs_l, ss_l, cs_l = [], [], [], [], [], [], [], []
    for l in range(DEPTH):
        w = (norm_mix_g[l], w_in[l], rel_bias[l], hgrn_norm_g[l], w_branch_a[l], w_branch_b[l], w_out[l],
             norm_ffn_g[l], w_ffn_gate[l], w_ffn_up[l], ffn_conv_w[l], ffn_conv_b[l], w_ffn_down[l])
        S0p = jnp.zeros((Bp, N_HEADS_B, HEAD_K_B, HEAD_V_B), jnp.float32)
        conv0 = jnp.zeros((Bp, CONV_W - 1, D_FF), xp.dtype)
        xp, kr, vr, Sp, cp = layer(xp, lbs[l], None, None, S0p, conv0, CHUNK, *w)
        kp_l.append(kr); vp_l.append(vr); sp_l.append(Sp.astype(xp.dtype)); cp_l.append(cp)
        xs, kr, vr, Ss, cs = layer(xs, lbs[l], cache_attn_k[l], cache_attn_v[l], state_hgrn[l],
                                   state_ffn_conv[l], xs.shape[1], *w)
        ks_l.append(kr); vs_l.append(vr); ss_l.append(Ss.astype(state_hgrn.dtype)); cs_l.append(cs)
    y_prompt = rms_norm(xp, norm_final_g)
    y_sample = rms_norm(xs, norm_final_g)
    return (y_prompt, y_sample,
            jnp.stack(kp_l), jnp.stack(vp_l), jnp.stack(sp_l), jnp.stack(cp_l),
            jnp.stack(ks_l), jnp.stack(vs_l), jnp.stack(ss_l), jnp.stack(cs_l))
```

```python
import functools

import jax
import jax.numpy as jnp
from jax import lax
from jax.experimental import pallas as pl
from jax.experimental.pallas import tpu as pltpu

F32 = jnp.float32
BF16 = jnp.bfloat16

D_MODEL = 1024
CHUNK = 64
N_PREV_CHUNKS = 8
ATT_WINDOW = N_PREV_CHUNKS * CHUNK
N_HEADS_A = 8
HEAD_DIM_A = 64
D_A = N_HEADS_A * HEAD_DIM_A
REL_CLIP = 128
N_REL = CHUNK + REL_CLIP
N_HEADS_B = 4
HEAD_K_B = 128
HEAD_V_B = 128
D_B = N_HEADS_B * HEAD_K_B
D_FF = 2816
CONV_W = 3
EPS = 1e-6
D_IN = 3 * D_A + 4 * D_B + 2 * D_MODEL
ATT_SCALE = HEAD_DIM_A ** -0.5

WIN = (N_PREV_CHUNKS + 2) * CHUNK
HEADS_PER_GROUP = 4
GROUP_LANES = HEADS_PER_GROUP * HEAD_DIM_A
NEG = -1e30

VMEM_LIMIT = 56 * 1024 * 1024


def _sigmoid(x):
    return 1.0 / (1.0 + jnp.exp(-x))


def _rms(x, g):
    return x * lax.rsqrt(jnp.mean(x * x, axis=-1, keepdims=True) + EPS) * g


def _dot(a, b):
    return jnp.dot(a, b, preferred_element_type=F32)


def _dot_nt(a, b):
    return lax.dot_general(a, b, (((1,), (1,)), ((), ())), preferred_element_type=F32)


def _dot_tn(a, b):
    return lax.dot_general(a, b, (((0,), (0,)), ((), ())), preferred_element_type=F32)


def _const_spec(shape):
    nd = len(shape)
    return pl.BlockSpec(shape, lambda *_: (0,) * nd, pipeline_mode=pl.Buffered(1))


def _inproj_kernel(x_ref, g_ref, w_ref, q_ref, k_ref, v_ref, hg_ref, zab_ref, kvf_ref, *,
                   tiles_per_batch, tail_tiles):
    h = _rms(x_ref[...], g_ref[...]).astype(BF16)

    def proj(c0, c1):
        return _dot(h, w_ref[:, c0:c1])

    q_ref[...] = (proj(0, D_A) * ATT_SCALE).astype(BF16)
    kf = proj(D_A, 2 * D_A)
    vf = proj(2 * D_A, 3 * D_A)
    k_ref[...] = kf.astype(BF16)
    v_ref[...] = vf.astype(BF16)
    off = 3 * D_A
    for j in range(4):
        hg_ref[:, j * D_B:(j + 1) * D_B] = proj(off + j * D_B, off + (j + 1) * D_B)
    off = 3 * D_A + 4 * D_B
    for j in range(4):
        zab_ref[:, j * 512:(j + 1) * 512] = proj(off + j * 512, off + (j + 1) * 512)

    def write_kv():
        kvf_ref[:, 0:D_A] = kf
        kvf_ref[:, D_A:2 * D_A] = vf

    if tail_tiles >= tiles_per_batch:
        write_kv()
    else:
        pl.when(pl.program_id(1) >= tiles_per_batch - tail_tiles)(write_kv)


def _inproj(x, g, w_bf, *, tm, keep_rows):
    B, L, _ = x.shape
    T = L // tm
    tail = keep_rows // tm
    assert T * tm == L and tail * tm == keep_rows and tail >= 1
    row = lambda w: pl.BlockSpec((None, tm, w), lambda b, t: (b, t, 0))
    kern = functools.partial(_inproj_kernel, tiles_per_batch=T, tail_tiles=tail)
    return pl.pallas_call(
        kern,
        grid=(B, T),
        in_specs=[row(D_MODEL), _const_spec((1, D_MODEL)), _const_spec((D_MODEL, D_IN))],
        out_specs=[row(D_A), row(D_A), row(D_A), row(4 * D_B), row(2 * D_MODEL),
                   pl.BlockSpec((None, tm, 2 * D_A), lambda b, t: (b, jnp.maximum(t - (T - tail), 0), 0))],
        out_shape=[jax.ShapeDtypeStruct((B, L, D_A), BF16)] * 3
        + [jax.ShapeDtypeStruct((B, L, 4 * D_B), F32), jax.ShapeDtypeStruct((B, L, 2 * D_MODEL), F32),
           jax.ShapeDtypeStruct((B, keep_rows, 2 * D_A), F32)],
        compiler_params=pltpu.CompilerParams(dimension_semantics=("arbitrary", "arbitrary"),
                                             vmem_limit_bytes=VMEM_LIMIT),
        name="inproj",
    )(x, g.reshape(1, D_MODEL), w_bf)


def _build_bias(relb_ref, bias_scr):
    col = lax.broadcasted_iota(jnp.int32, (N_HEADS_A, WIN), 1)
    tab = relb_ref[...]
    base = jnp.broadcast_to(tab[:, N_REL - 1:N_REL], (N_HEADS_A, WIN))
    for t in range(N_REL - 1):
        base = jnp.where(col == (WIN - 1 - t), tab[:, t:t + 1], base)
    row = lax.broadcasted_iota(jnp.int32, (CHUNK, WIN), 0)
    colq = lax.broadcasted_iota(jnp.int32, (CHUNK, WIN), 1)
    for h in range(N_HEADS_A):
        x = jnp.broadcast_to(base[h:h + 1, :], (CHUNK, WIN))
        shift = 1
        while shift < CHUNK:
            x = jnp.where((row & shift) != 0, pltpu.roll(x, shift, 1), x)
            shift *= 2
        bias_scr[h] = jnp.where(colq >= CHUNK, x, NEG)


def _attend_chunk(qc, kw, vw, bias_scr, mrow):
    lane_head = lax.broadcasted_iota(jnp.int32, (CHUNK, GROUP_LANES), 1) // HEAD_DIM_A
    outs = []
    for g in range(N_HEADS_A // HEADS_PER_GROUP):
        ls = slice(g * GROUP_LANES, (g + 1) * GROUP_LANES)
        q256 = qc[:, ls]
        qs = jnp.concatenate(
            [jnp.where(lane_head == hh, q256, jnp.zeros_like(q256)) for hh in range(HEADS_PER_GROUP)], axis=0)
        s = _dot_nt(qs, kw[:, ls])
        bias = jnp.concatenate([bias_scr[g * HEADS_PER_GROUP + hh] for hh in range(HEADS_PER_GROUP)], axis=0)
        s = s + bias + mrow
        m = jnp.max(s, axis=-1, keepdims=True)
        p = jnp.exp(s - m)
        l = jnp.sum(p, axis=-1, keepdims=True)
        pv = _dot(p.astype(BF16), vw[:, ls]) * (1.0 / l)
        o = jnp.zeros((CHUNK, GROUP_LANES), F32)
        for hh in range(HEADS_PER_GROUP):
            o = jnp.where(lane_head == hh, pv[hh * CHUNK:(hh + 1) * CHUNK, :], o)
        outs.append(o)
    return jnp.concatenate(outs, axis=1)


def _attn_prompt_kernel(relb_ref, q_ref, k_ref, v_ref, o_ref, kbuf, vbuf, bias_scr, *, qb):
    i = pl.program_id(1)
    hist = WIN - CHUNK

    @pl.when((pl.program_id(0) == 0) & (i == 0))
    def _():
        _build_bias(relb_ref, bias_scr)

    @pl.when(i == 0)
    def _():
        kbuf[0:hist, :] = jnp.zeros((hist, D_A), BF16)
        vbuf[0:hist, :] = jnp.zeros((hist, D_A), BF16)

    @pl.when(i > 0)
    def _():
        kbuf[0:hist, :] = kbuf[qb:qb + hist, :]
        vbuf[0:hist, :] = vbuf[qb:qb + hist, :]

    kbuf[hist:hist + qb, :] = k_ref[...]
    vbuf[hist:hist + qb, :] = v_ref[...]
    col = lax.broadcasted_iota(jnp.int32, (1, WIN), 1)

    def body(cl, carry):
        r0 = pl.multiple_of(cl * CHUNK, CHUNK)
        lo = jnp.where(i == 0, hist - cl * CHUNK, 0)
        mrow = jnp.where(col >= lo, 0.0, NEG).astype(F32)
        o = _attend_chunk(q_ref[pl.ds(r0, CHUNK), :], kbuf[pl.ds(r0, WIN), :], vbuf[pl.ds(r0, WIN), :],
                          bias_scr, mrow)
        o_ref[pl.ds(r0, CHUNK), :] = o.astype(BF16)
        return carry

    lax.fori_loop(0, qb // CHUNK, body, 0)


def _attn_prompt(relb, q, k, v, *, qb):
    B, L, _ = q.shape
    blk = pl.BlockSpec((None, qb, D_A), lambda b, i: (b, i, 0))
    return pl.pallas_call(
        functools.partial(_attn_prompt_kernel, qb=qb),
        grid=(B, L // qb),
        in_specs=[_const_spec((N_HEADS_A, N_REL)), blk, blk, blk],
        out_specs=blk,
        out_shape=jax.ShapeDtypeStruct((B, L, D_A), BF16),
        scratch_shapes=[pltpu.VMEM((WIN - CHUNK + qb, D_A), BF16), pltpu.VMEM((WIN - CHUNK + qb, D_A), BF16),
                        pltpu.VMEM((N_HEADS_A, CHUNK, WIN), F32)],
        compiler_params=pltpu.CompilerParams(dimension_semantics=("arbitrary", "arbitrary"),
                                             vmem_limit_bytes=VMEM_LIMIT),
        name="attn_prompt",
    )(relb, q, k, v)


def _attn_window_kernel(relb_ref, q_ref, k_ref, v_ref, o_ref, bias_scr, *, n_valid):
    @pl.when(pl.program_id(0) == 0)
    def _():
        _build_bias(relb_ref, bias_scr)

    col = lax.broadcasted_iota(jnp.int32, (1, WIN), 1)
    mrow = jnp.where(col < n_valid, 0.0, NEG).astype(F32)
    o_ref[...] = _attend_chunk(q_ref[...], k_ref[...], v_ref[...], bias_scr, mrow).astype(BF16)


def _attn_window(relb, qwin, kwin, vwin, *, n_valid):
    B = qwin.shape[0]
    return pl.pallas_call(
        functools.partial(_attn_window_kernel, n_valid=n_valid),
        grid=(B,),
        in_specs=[_const_spec((N_HEADS_A, N_REL)),
                  pl.BlockSpec((None, CHUNK, D_A), lambda b: (b, 0, 0)),
                  pl.BlockSpec((None, WIN, D_A), lambda b: (b, 0, 0)),
                  pl.BlockSpec((None, WIN, D_A), lambda b: (b, 0, 0))],
        out_specs=pl.BlockSpec((None, CHUNK, D_A), lambda b: (b, 0, 0)),
        out_shape=jax.ShapeDtypeStruct((B, CHUNK, D_A), BF16),
        scratch_shapes=[pltpu.VMEM((N_HEADS_A, CHUNK, WIN), F32)],
        compiler_params=pltpu.CompilerParams(dimension_semantics=("arbitrary",), vmem_limit_bytes=VMEM_LIMIT),
        name="attn_window",
    )(relb, qwin, kwin, vwin)


def _block_bcast(x, period, src):
    R, C = x.shape
    if period % 8 == 0:
        x3 = x.reshape(R // period, period, C)
        return jnp.broadcast_to(x3[:, src:src + 1, :], x3.shape).reshape(R, C)
    off = lax.broadcasted_iota(jnp.int32, (R, C), 0) % period
    out = x
    for j in range(period):
        if j != src:
            out = jnp.where(off == j, pltpu.roll(x, (j - src) % R, 0), out)
    return out


def _hgrn_kernel(*refs, tb, chunk, has_s0):
    if has_s0:
        hg_ref, lbl_ref, gn_ref, s0_ref, ob_ref, sout_ref, st_scr, z_scr, q_scr, k_scr, qi_scr, ks_scr, v_scr, dec_scr = refs
    else:
        hg_ref, lbl_ref, gn_ref, ob_ref, sout_ref, st_scr, z_scr, q_scr, k_scr, qi_scr, ks_scr, v_scr, dec_scr = refs
    i = pl.program_id(1)

    @pl.when(i == 0)
    def _():
        for h in range(N_HEADS_B):
            if has_s0:
                st_scr[h] = s0_ref[h].T
            else:
                st_scr[h] = jnp.zeros((HEAD_V_B, HEAD_K_B), F32)

    lg = lbl_ref[...]
    e = jnp.exp(lg - jnp.max(lg, axis=0, keepdims=True))
    lb = e[0:1, :] / jnp.sum(e, axis=0, keepdims=True)

    qb = hg_ref[:, 0:D_B]
    fb = hg_ref[:, D_B:2 * D_B]
    f = lb + (1.0 - lb) * _sigmoid(fb)
    q = qb * _sigmoid(qb)
    k = 1.0 - f
    rc = lax.broadcasted_iota(jnp.int32, (tb, D_B), 0) % chunk
    b = jnp.log(f)
    s = 1
    while s < chunk:
        b = b + jnp.where(rc >= s, pltpu.roll(b, s, 0), 0.0)
        s *= 2
    b_end = _block_bcast(b, chunk, chunk - 1)
    q_scr[...] = q.astype(BF16)
    k_scr[...] = k.astype(BF16)
    qi_scr[...] = (q * jnp.exp(b)).astype(BF16)
    ks_scr[...] = (k * jnp.exp(b_end - b)).astype(BF16)
    v_scr[...] = hg_ref[:, 2 * D_B:3 * D_B].astype(BF16)
    dec_scr[...] = jnp.exp(b_end)
    levels = []
    m = chunk // 2
    while m >= 1:
        levels.append(m)
        m //= 2
    for li, m in enumerate(levels):
        edge = _block_bcast(b, 2 * m, m - 1)
        left = (rc % (2 * m)) < m
        w = jnp.exp(jnp.where(left, edge - b, b - edge))
        z_scr[li] = (jnp.where(left, k, q) * w).astype(BF16)

    t_id = lax.broadcasted_iota(jnp.int32, (chunk, chunk), 0)
    s_id = lax.broadcasted_iota(jnp.int32, (chunk, chunk), 1)
    diff = t_id ^ s_id
    on_diag = diff == 0
    lev_masks = [jnp.where(t_id > s_id, diff, 0) // m == 1 for m in levels]
    gn = gn_ref[...]

    def body(c, carry):
        r0 = pl.multiple_of(c * chunk, chunk)
        rows = pl.ds(r0, chunk)
        for h in range(N_HEADS_B):
            ls = slice(h * HEAD_K_B, (h + 1) * HEAD_K_B)
            a = jnp.where(on_diag, _dot_nt(q_scr[rows, ls], k_scr[rows, ls]), 0.0)
            for li in range(len(levels)):
                zc = z_scr[li, rows, ls]
                a = jnp.where(lev_masks[li], _dot_nt(zc, zc), a)
            st = st_scr[h]
            vc = v_scr[rows, ls]
            o = _dot_nt(qi_scr[rows, ls], st.astype(BF16)) + _dot(a.astype(BF16), vc)
            st_scr[h] = st * dec_scr[pl.ds(r0, 1), ls] + _dot_tn(vc, ks_scr[rows, ls])
            on = _rms(o, gn)
            gb = hg_ref[rows, 3 * D_B + h * HEAD_V_B:3 * D_B + (h + 1) * HEAD_V_B]
            ob_ref[rows, ls] = (on * (gb * _sigmoid(gb))).astype(BF16)
        return carry

    lax.fori_loop(0, tb // chunk, body, 0)

    @pl.when(i == pl.num_programs(1) - 1)
    def _():
        for h in range(N_HEADS_B):
            sout_ref[h] = st_scr[h].T


def _hgrn(hg, lb_logits, g_norm, s0, *, tb, chunk):
    B, L, _ = hg.shape
    has_s0 = s0 is not None
    n_lev = chunk.bit_length() - 1
    st_spec = pl.BlockSpec((None, N_HEADS_B, HEAD_K_B, HEAD_V_B), lambda b, i: (b, 0, 0, 0))
    in_specs = [pl.BlockSpec((None, tb, 4 * D_B), lambda b, i: (b, i, 0)),
                _const_spec(lb_logits.shape), _const_spec((1, HEAD_V_B))]
    args = [hg, lb_logits, g_norm.reshape(1, HEAD_V_B)]
    if has_s0:
        in_specs.append(st_spec)
        args.append(s0)
    act = lambda: pltpu.VMEM((tb, D_B), BF16)
    return pl.pallas_call(
        functools.partial(_hgrn_kernel, tb=tb, chunk=chunk, has_s0=has_s0),
        grid=(B, L // tb),
        in_specs=in_specs,
        out_specs=[pl.BlockSpec((None, tb, D_B), lambda b, i: (b, i, 0)), st_spec],
        out_shape=[jax.ShapeDtypeStruct((B, L, D_B), BF16),
                   jax.ShapeDtypeStruct((B, N_HEADS_B, HEAD_K_B, HEAD_V_B), F32)],
        scratch_shapes=[pltpu.VMEM((N_HEADS_B, HEAD_V_B, HEAD_K_B), F32),
                        pltpu.VMEM((n_lev, tb, D_B), BF16),
                        act(), act(), act(), act(), act(),
                        pltpu.VMEM((tb, D_B), F32)],
        compiler_params=pltpu.CompilerParams(dimension_semantics=("arbitrary", "arbitrary"),
                                             vmem_limit_bytes=VMEM_LIMIT),
        name="hgrn",
    )(*args)


def _gelu_tanh(x):
    return 0.5 * x * (1.0 + jnp.tanh(0.7978845608028654 * (x + 0.044715 * (x * x * x))))


def _mlp_kernel(*refs, tm, seg, has_state):
    if has_state:
        (x_ref, oa_ref, ob_ref, zab_ref, cst_ref, wa_ref, wb_ref, wo_ref, g2_ref, wg_ref, wu_ref, cw_ref, cb_ref,
         wd_ref, gf_ref, y_ref, cnew_ref, a_scr) = refs
    else:
        (x_ref, oa_ref, ob_ref, zab_ref, wa_ref, wb_ref, wo_ref, g2_ref, wg_ref, wu_ref, cw_ref, cb_ref,
         wd_ref, gf_ref, y_ref, cnew_ref, a_scr) = refs
    ma = _dot(oa_ref[...], wa_ref[...])
    mb = _dot(ob_ref[...], wb_ref[...])
    merged = _sigmoid(zab_ref[:, 0:D_MODEL]) * ma + _sigmoid(zab_ref[:, D_MODEL:2 * D_MODEL]) * mb
    x1 = x_ref[...] + _dot(merged.astype(BF16), wo_ref[...])
    h2 = _rms(x1, g2_ref[...]).astype(BF16)
    a = _dot(h2, wg_ref[...])
    u = _dot(h2, wu_ref[...])
    cw0, cw1, cw2 = cw_ref[0:1, :], cw_ref[1:2, :], cw_ref[2:3, :]
    cb = cb_ref[...]
    if has_state:
        nseg = tm // seg
        parts = []
        for j in range(nseg):
            a_scr[j, 6:8, :] = cst_ref[j]
            a_scr[j, 8:8 + seg, :] = a[j * seg:(j + 1) * seg, :]
            parts.append(cb + cw0 * a_scr[j, 6:6 + seg, :] + cw1 * a_scr[j, 7:7 + seg, :]
                         + cw2 * a[j * seg:(j + 1) * seg, :])
            cnew_ref[j] = a_scr[j, 6 + seg:8 + seg, :]
        ac = jnp.concatenate(parts, axis=0)
    else:
        t = pl.program_id(1)

        @pl.when(t == 0)
        def _():
            a_scr[0:8, :] = jnp.zeros((8, D_FF), F32)

        a_scr[8:8 + tm, :] = a
        ac = cb + cw0 * a_scr[6:6 + tm, :] + cw1 * a_scr[7:7 + tm, :] + cw2 * a
        a_scr[0:8, :] = a_scr[tm:tm + 8, :]

        @pl.when(t == pl.num_programs(1) - 1)
        def _():
            cnew_ref[...] = a_scr[6:8, :]

    g = (_gelu_tanh(ac) * u).astype(BF16)
    y = x1 + _dot(g, wd_ref[...])
    y_ref[...] = _rms(y, gf_ref[...])


def _mlp(x, oa, ob, zab, conv_state, wa, wb, wo, g2, wg, wu, cw, cb, wd, gf, *, tm):
    B, L, _ = x.shape
    has_state = conv_state is not None
    row = lambda w: pl.BlockSpec((None, tm, w), lambda b, t: (b, t, 0))
    in_specs = [row(D_MODEL), row(D_A), row(D_B), row(2 * D_MODEL)]
    args = [x, oa, ob, zab]
    if has_state:
        nseg = conv_state.shape[0]
        seg = L // nseg
        assert tm == L and B == 1
        in_specs.append(_const_spec(conv_state.shape))
        args.append(conv_state)
        cnew_shape = conv_state.shape
        cnew_spec = pl.BlockSpec(cnew_shape, lambda b, t: (0, 0, 0))
        scratch = pltpu.VMEM((nseg, seg + 8, D_FF), F32)
    else:
        seg = tm
        cnew_shape = (B, CONV_W - 1, D_FF)
        cnew_spec = pl.BlockSpec((None, CONV_W - 1, D_FF), lambda b, t: (b, 0, 0))
        scratch = pltpu.VMEM((tm + 8, D_FF), F32)
    weights = [wa, wb, wo, g2.reshape(1, D_MODEL), wg, wu, cw, cb.reshape(1, D_FF), wd, gf.reshape(1, D_MODEL)]
    in_specs += [_const_spec(w.shape) for w in weights]
    return pl.pallas_call(
        functools.partial(_mlp_kernel, tm=tm, seg=seg, has_state=has_state),
        grid=(B, L // tm),
        in_specs=in_specs,
        out_specs=[row(D_MODEL), cnew_spec],
        out_shape=[jax.ShapeDtypeStruct((B, L, D_MODEL), F32), jax.ShapeDtypeStruct(cnew_shape, F32)],
        scratch_shapes=[scratch],
        compiler_params=pltpu.CompilerParams(dimension_semantics=("arbitrary", "arbitrary"),
                                             vmem_limit_bytes=VMEM_LIMIT),
        name="mlp",
    )(*args, *weights)


def _heads(t, n_heads):
    B, L, D = t.shape
    return t.reshape(B, L, n_heads, D // n_heads).transpose(0, 2, 1, 3)


def kernel(x_prompt, x_sample, cache_attn_k, cache_attn_v, state_hgrn, state_ffn_conv, norm_mix_g, w_in, rel_bias,
           hgrn_lb_logits, hgrn_norm_g, w_branch_a, w_branch_b, w_out, norm_ffn_g, w_ffn_gate, w_ffn_up,
           ffn_conv_w, ffn_conv_b, w_ffn_down, norm_final_g):
    depth = w_in.shape[0]
    assert depth == 1, "single-layer step"
    Bp, Lp, _ = x_prompt.shape
    Bs, Ls, _ = x_sample.shape
    w_in_bf = w_in[0].astype(BF16)
    wa, wb, wo = w_branch_a[0].astype(BF16), w_branch_b[0].astype(BF16), w_out[0].astype(BF16)
    wg, wu, wd = w_ffn_gate[0].astype(BF16), w_ffn_up[0].astype(BF16), w_ffn_down[0].astype(BF16)
    mlp_w = (wa, wb, wo, norm_ffn_g[0], wg, wu, ffn_conv_w[0], ffn_conv_b[0], wd, norm_final_g)

    keep = min(ATT_WINDOW, Lp)
    q, k, v, hg, zab, kvf = _inproj(x_prompt, norm_mix_g[0], w_in_bf, tm=512, keep_rows=keep)
    oa = _attn_prompt(rel_bias[0], q, k, v, qb=512)
    ob, s_p = _hgrn(hg, hgrn_lb_logits, hgrn_norm_g[0], None, tb=512, chunk=CHUNK)
    y_p, conv_p = _mlp(x_prompt, oa, ob, zab, None, *mlp_w, tm=256)
    k_p = _heads(kvf[:, :, :D_A], N_HEADS_A)
    v_p = _heads(kvf[:, :, D_A:], N_HEADS_A)

    n_s = Bs * Ls
    xs = x_sample.reshape(1, n_s, D_MODEL)
    q, k, v, hg, zab, kvf = _inproj(xs, norm_mix_g[0], w_in_bf, tm=n_s, keep_rows=n_s)
    w_att = cache_attn_k.shape[3]
    assert w_att == ATT_WINDOW and Ls <= CHUNK
    pad_q = jnp.zeros((Bs, CHUNK - Ls, D_A), BF16)

    def window(cache, new):
        past = cache[0].astype(BF16).transpose(0, 2, 1, 3).reshape(Bs, w_att, D_A)
        return jnp.concatenate([jnp.zeros((Bs, CHUNK, D_A), BF16), past, new.reshape(Bs, Ls, D_A), pad_q], axis=1)

    qwin = jnp.concatenate([q.reshape(Bs, Ls, D_A), pad_q], axis=1)
    oa = _attn_window(rel_bias[0], qwin, window(cache_attn_k, k), window(cache_attn_v, v),
                      n_valid=CHUNK + w_att + Ls)[:, :Ls].reshape(1, n_s, D_A)
    ob, s_s = _hgrn(hg.reshape(Bs, Ls, 4 * D_B), hgrn_lb_logits, hgrn_norm_g[0], state_hgrn[0], tb=Ls, chunk=Ls)
    y_s, conv_s = _mlp(xs, oa, ob.reshape(1, n_s, D_B), zab, state_ffn_conv[0], *mlp_w, tm=n_s)
    kvf = kvf.reshape(Bs, Ls, 2 * D_A)
    k_s = _heads(kvf[:, :, :D_A], N_HEADS_A)
    v_s = _heads(kvf[:, :, D_A:], N_HEADS_A)

    return (y_p, y_s.reshape(Bs, Ls, D_MODEL),
            k_p[None], v_p[None], s_p[None], conv_p[None],
            k_s[None], v_s[None], s_s[None], conv_s[None])
```

```python
import functools

import jax
import jax.numpy as jnp
from jax import lax
from jax.experimental import pallas as pl
from jax.experimental.pallas import tpu as pltpu

F32 = jnp.float32
BF16 = jnp.bfloat16

D_MODEL = 1024
CHUNK = 64
N_PREV_CHUNKS = 8
ATT_WINDOW = N_PREV_CHUNKS * CHUNK
N_HEADS_A = 8
HEAD_DIM_A = 64
D_A = N_HEADS_A * HEAD_DIM_A
REL_CLIP = 128
N_REL = CHUNK + REL_CLIP
N_HEADS_B = 4
HEAD_K_B = 128
HEAD_V_B = 128
D_B = N_HEADS_B * HEAD_K_B
D_FF = 2816
CONV_W = 3
EPS = 1e-6
D_IN = 3 * D_A + 4 * D_B + 2 * D_MODEL
ATT_SCALE = HEAD_DIM_A ** -0.5

WIN = (N_PREV_CHUNKS + 2) * CHUNK
HEADS_PER_GROUP = 4
GROUP_LANES = HEADS_PER_GROUP * HEAD_DIM_A
NEG = -1e30

VMEM_LIMIT = 56 * 1024 * 1024


def _sigmoid(x):
    return 1.0 / (1.0 + jnp.exp(-x))


def _rms(x, g):
    return x * lax.rsqrt(jnp.mean(x * x, axis=-1, keepdims=True) + EPS) * g


def _dot(a, b):
    return jnp.dot(a, b, preferred_element_type=F32)


def _dot_nt(a, b):
    return lax.dot_general(a, b, (((1,), (1,)), ((), ())), preferred_element_type=F32)


def _dot_tn(a, b):
    return lax.dot_general(a, b, (((0,), (0,)), ((), ())), preferred_element_type=F32)


def _const_spec(shape):
    nd = len(shape)
    return pl.BlockSpec(shape, lambda *_: (0,) * nd, pipeline_mode=pl.Buffered(1))


def _inproj_kernel(x_ref, g_ref, w_ref, q_ref, k_ref, v_ref, hg_ref, zab_ref, kvf_ref, *,
                   tiles_per_batch, tail_tiles):
    h = _rms(x_ref[...], g_ref[...]).astype(BF16)

    def proj(c0, c1):
        return _dot(h, w_ref[:, c0:c1])

    q_ref[...] = (proj(0, D_A) * ATT_SCALE).astype(BF16)
    kf = proj(D_A, 2 * D_A)
    vf = proj(2 * D_A, 3 * D_A)
    k_ref[...] = kf.astype(BF16)
    v_ref[...] = vf.astype(BF16)
    off = 3 * D_A
    for j in range(4):
        hg_ref[:, j * D_B:(j + 1) * D_B] = proj(off + j * D_B, off + (j + 1) * D_B)
    off = 3 * D_A + 4 * D_B
    for j in range(4):
        zab_ref[:, j * 512:(j + 1) * 512] = proj(off + j * 512, off + (j + 1) * 512).astype(BF16)

    def write_kv():
        kvf_ref[:, 0:D_A] = kf
        kvf_ref[:, D_A:2 * D_A] = vf

    if tail_tiles >= tiles_per_batch:
        write_kv()
    else:
        pl.when(pl.program_id(1) >= tiles_per_batch - tail_tiles)(write_kv)


def _inproj(x, g, w_bf, *, tm, keep_rows):
    B, L, _ = x.shape
    T = L // tm
    tail = keep_rows // tm
    assert T * tm == L and tail * tm == keep_rows and tail >= 1
    row = lambda w: pl.BlockSpec((None, tm, w), lambda b, t: (b, t, 0))
    kern = functools.partial(_inproj_kernel, tiles_per_batch=T, tail_tiles=tail)
    return pl.pallas_call(
        kern,
        grid=(B, T),
        in_specs=[row(D_MODEL), _const_spec((1, D_MODEL)), _const_spec((D_MODEL, D_IN))],
        out_specs=[row(D_A), row(D_A), row(D_A), row(4 * D_B), row(2 * D_MODEL),
                   pl.BlockSpec((None, tm, 2 * D_A), lambda b, t: (b, jnp.maximum(t - (T - tail), 0), 0))],
        out_shape=[jax.ShapeDtypeStruct((B, L, D_A), BF16)] * 3
        + [jax.ShapeDtypeStruct((B, L, 4 * D_B), F32), jax.ShapeDtypeStruct((B, L, 2 * D_MODEL), BF16),
           jax.ShapeDtypeStruct((B, keep_rows, 2 * D_A), F32)],
        compiler_params=pltpu.CompilerParams(dimension_semantics=("arbitrary", "arbitrary"),
                                             vmem_limit_bytes=VMEM_LIMIT),
        name="inproj",
    )(x, g.reshape(1, D_MODEL), w_bf)


def _build_bias(relb_ref, bias_scr):
    col = lax.broadcasted_iota(jnp.int32, (N_HEADS_A, WIN), 1)
    tab = relb_ref[...]
    base = jnp.broadcast_to(tab[:, N_REL - 1:N_REL], (N_HEADS_A, WIN))
    for t in range(N_REL - 1):
        base = jnp.where(col == (WIN - 1 - t), tab[:, t:t + 1], base)
    row = lax.broadcasted_iota(jnp.int32, (CHUNK, WIN), 0)
    colq = lax.broadcasted_iota(jnp.int32, (CHUNK, WIN), 1)
    for h in range(N_HEADS_A):
        x = jnp.broadcast_to(base[h:h + 1, :], (CHUNK, WIN))
        shift = 1
        while shift < CHUNK:
            x = jnp.where((row & shift) != 0, pltpu.roll(x, shift, 1), x)
            shift *= 2
        bias_scr[h] = jnp.where(colq >= CHUNK, x, NEG)


def _attend_chunk(qc, kw, vw, bias_scr, mrow):
    lane_head = lax.broadcasted_iota(jnp.int32, (CHUNK, GROUP_LANES), 1) // HEAD_DIM_A
    outs = []
    for g in range(N_HEADS_A // HEADS_PER_GROUP):
        ls = slice(g * GROUP_LANES, (g + 1) * GROUP_LANES)
        q256 = qc[:, ls]
        qs = jnp.concatenate(
            [jnp.where(lane_head == hh, q256, jnp.zeros_like(q256)) for hh in range(HEADS_PER_GROUP)], axis=0)
        s = _dot_nt(qs, kw[:, ls])
        bias = jnp.concatenate([bias_scr[g * HEADS_PER_GROUP + hh] for hh in range(HEADS_PER_GROUP)], axis=0)
        s = s + bias + mrow
        m = jnp.max(s, axis=-1, keepdims=True)
        p = jnp.exp(s - m)
        l = jnp.sum(p, axis=-1, keepdims=True)
        pv = _dot(p.astype(BF16), vw[:, ls]) * (1.0 / l)
        o = jnp.zeros((CHUNK, GROUP_LANES), F32)
        for hh in range(HEADS_PER_GROUP):
            o = jnp.where(lane_head == hh, pv[hh * CHUNK:(hh + 1) * CHUNK, :], o)
        outs.append(o)
    return jnp.concatenate(outs, axis=1)


def _attn_prompt_kernel(relb_ref, q_ref, k_ref, v_ref, o_ref, kbuf, vbuf, bias_scr, *, qb):
    i = pl.program_id(1)
    hist = WIN - CHUNK

    @pl.when((pl.program_id(0) == 0) & (i == 0))
    def _():
        _build_bias(relb_ref, bias_scr)

    @pl.when(i == 0)
    def _():
        kbuf[0:hist, :] = jnp.zeros((hist, D_A), BF16)
        vbuf[0:hist, :] = jnp.zeros((hist, D_A), BF16)

    @pl.when(i > 0)
    def _():
        kbuf[0:hist, :] = kbuf[qb:qb + hist, :]
        vbuf[0:hist, :] = vbuf[qb:qb + hist, :]

    kbuf[hist:hist + qb, :] = k_ref[...]
    vbuf[hist:hist + qb, :] = v_ref[...]
    col = lax.broadcasted_iota(jnp.int32, (1, WIN), 1)

    def body(cl, carry):
        r0 = pl.multiple_of(cl * CHUNK, CHUNK)
        lo = jnp.where(i == 0, hist - cl * CHUNK, 0)
        mrow = jnp.where(col >= lo, 0.0, NEG).astype(F32)
        o = _attend_chunk(q_ref[pl.ds(r0, CHUNK), :], kbuf[pl.ds(r0, WIN), :], vbuf[pl.ds(r0, WIN), :],
                          bias_scr, mrow)
        o_ref[pl.ds(r0, CHUNK), :] = o.astype(BF16)
        return carry

    lax.fori_loop(0, qb // CHUNK, body, 0)


def _attn_prompt(relb, q, k, v, *, qb):
    B, L, _ = q.shape
    blk = pl.BlockSpec((None, qb, D_A), lambda b, i: (b, i, 0))
    return pl.pallas_call(
        functools.partial(_attn_prompt_kernel, qb=qb),
        grid=(B, L // qb),
        in_specs=[_const_spec((N_HEADS_A, N_REL)), blk, blk, blk],
        out_specs=blk,
        out_shape=jax.ShapeDtypeStruct((B, L, D_A), BF16),
        scratch_shapes=[pltpu.VMEM((WIN - CHUNK + qb, D_A), BF16), pltpu.VMEM((WIN - CHUNK + qb, D_A), BF16),
                        pltpu.VMEM((N_HEADS_A, CHUNK, WIN), F32)],
        compiler_params=pltpu.CompilerParams(dimension_semantics=("arbitrary", "arbitrary"),
                                             vmem_limit_bytes=VMEM_LIMIT),
        name="attn_prompt",
    )(relb, q, k, v)


def _attn_window_kernel(relb_ref, q_ref, k_ref, v_ref, o_ref, bias_scr, *, n_valid):
    @pl.when(pl.program_id(0) == 0)
    def _():
        _build_bias(relb_ref, bias_scr)

    col = lax.broadcasted_iota(jnp.int32, (1, WIN), 1)
    mrow = jnp.where(col < n_valid, 0.0, NEG).astype(F32)
    o_ref[...] = _attend_chunk(q_ref[...], k_ref[...], v_ref[...], bias_scr, mrow).astype(BF16)


def _attn_window(relb, qwin, kwin, vwin, *, n_valid):
    B = qwin.shape[0]
    return pl.pallas_call(
        functools.partial(_attn_window_kernel, n_valid=n_valid),
        grid=(B,),
        in_specs=[_const_spec((N_HEADS_A, N_REL)),
                  pl.BlockSpec((None, CHUNK, D_A), lambda b: (b, 0, 0)),
                  pl.BlockSpec((None, WIN, D_A), lambda b: (b, 0, 0)),
                  pl.BlockSpec((None, WIN, D_A), lambda b: (b, 0, 0))],
        out_specs=pl.BlockSpec((None, CHUNK, D_A), lambda b: (b, 0, 0)),
        out_shape=jax.ShapeDtypeStruct((B, CHUNK, D_A), BF16),
        scratch_shapes=[pltpu.VMEM((N_HEADS_A, CHUNK, WIN), F32)],
        compiler_params=pltpu.CompilerParams(dimension_semantics=("arbitrary",), vmem_limit_bytes=VMEM_LIMIT),
        name="attn_window",
    )(relb, qwin, kwin, vwin)


def _block_bcast(x, period, src):
    R, C = x.shape
    if period % 8 == 0:
        x3 = x.reshape(R // period, period, C)
        return jnp.broadcast_to(x3[:, src:src + 1, :], x3.shape).reshape(R, C)
    off = lax.broadcasted_iota(jnp.int32, (R, C), 0) % period
    out = x
    for j in range(period):
        if j != src:
            out = jnp.where(off == j, pltpu.roll(x, (j - src) % R, 0), out)
    return out


def _hgrn_kernel(*refs, tb, chunk, has_s0):
    if has_s0:
        hg_ref, lbl_ref, gn_ref, s0_ref, ob_ref, sout_ref, st_scr, z_scr, q_scr, k_scr, qi_scr, ks_scr, v_scr, dec_scr = refs
    else:
        hg_ref, lbl_ref, gn_ref, ob_ref, sout_ref, st_scr, z_scr, q_scr, k_scr, qi_scr, ks_scr, v_scr, dec_scr = refs
    i = pl.program_id(1)

    @pl.when(i == 0)
    def _():
        for h in range(N_HEADS_B):
            if has_s0:
                st_scr[h] = s0_ref[h].T
            else:
                st_scr[h] = jnp.zeros((HEAD_V_B, HEAD_K_B), F32)

    lg = lbl_ref[...]
    e = jnp.exp(lg - jnp.max(lg, axis=0, keepdims=True))
    lb = e[0:1, :] / jnp.sum(e, axis=0, keepdims=True)

    qb = hg_ref[:, 0:D_B]
    fb = hg_ref[:, D_B:2 * D_B]
    f = lb + (1.0 - lb) * _sigmoid(fb)
    q = qb * _sigmoid(qb)
    k = 1.0 - f
    rc = lax.broadcasted_iota(jnp.int32, (tb, D_B), 0) % chunk
    b = jnp.log(f)
    s = 1
    while s < chunk:
        b = b + jnp.where(rc >= s, pltpu.roll(b, s, 0), 0.0)
        s *= 2
    b_end = _block_bcast(b, chunk, chunk - 1)
    q_scr[...] = q.astype(BF16)
    k_scr[...] = k.astype(BF16)
    qi_scr[...] = (q * jnp.exp(b)).astype(BF16)
    ks_scr[...] = (k * jnp.exp(b_end - b)).astype(BF16)
    v_scr[...] = hg_ref[:, 2 * D_B:3 * D_B].astype(BF16)
    dec_scr[...] = jnp.exp(b_end)
    levels = []
    m = chunk // 2
    while m >= 1:
        levels.append(m)
        m //= 2
    for li, m in enumerate(levels):
        edge = _block_bcast(b, 2 * m, m - 1)
        left = (rc % (2 * m)) < m
        w = jnp.exp(jnp.where(left, edge - b, b - edge))
        z_scr[li] = (jnp.where(left, k, q) * w).astype(BF16)

    t_id = lax.broadcasted_iota(jnp.int32, (chunk, chunk), 0)
    s_id = lax.broadcasted_iota(jnp.int32, (chunk, chunk), 1)
    diff = t_id ^ s_id
    on_diag = diff == 0
    lev_masks = [jnp.where(t_id > s_id, diff, 0) // m == 1 for m in levels]
    gn = gn_ref[...]

    def body(c, carry):
        r0 = pl.multiple_of(c * chunk, chunk)
        rows = pl.ds(r0, chunk)
        for h in range(N_HEADS_B):
            ls = slice(h * HEAD_K_B, (h + 1) * HEAD_K_B)
            a = jnp.where(on_diag, _dot_nt(q_scr[rows, ls], k_scr[rows, ls]), 0.0)
            for li in range(len(levels)):
                zc = z_scr[li, rows, ls]
                a = jnp.where(lev_masks[li], _dot_nt(zc, zc), a)
            st = st_scr[h]
            vc = v_scr[rows, ls]
            o = _dot_nt(qi_scr[rows, ls], st.astype(BF16)) + _dot(a.astype(BF16), vc)
            st_scr[h] = st * dec_scr[pl.ds(r0, 1), ls] + _dot_tn(vc, ks_scr[rows, ls])
            on = _rms(o, gn)
            gb = hg_ref[rows, 3 * D_B + h * HEAD_V_B:3 * D_B + (h + 1) * HEAD_V_B]
            ob_ref[rows, ls] = (on * (gb * _sigmoid(gb))).astype(BF16)
        return carry

    lax.fori_loop(0, tb // chunk, body, 0)

    @pl.when(i == pl.num_programs(1) - 1)
    def _():
        for h in range(N_HEADS_B):
            sout_ref[h] = st_scr[h].T


def _hgrn(hg, lb_logits, g_norm, s0, *, tb, chunk):
    B, L, _ = hg.shape
    has_s0 = s0 is not None
    n_lev = chunk.bit_length() - 1
    st_spec = pl.BlockSpec((None, N_HEADS_B, HEAD_K_B, HEAD_V_B), lambda b, i: (b, 0, 0, 0))
    in_specs = [pl.BlockSpec((None, tb, 4 * D_B), lambda b, i: (b, i, 0)),
                _const_spec(lb_logits.shape), _const_spec((1, HEAD_V_B))]
    args = [hg, lb_logits, g_norm.reshape(1, HEAD_V_B)]
    if has_s0:
        in_specs.append(st_spec)
        args.append(s0)
    act = lambda: pltpu.VMEM((tb, D_B), BF16)
    return pl.pallas_call(
        functools.partial(_hgrn_kernel, tb=tb, chunk=chunk, has_s0=has_s0),
        grid=(B, L // tb),
        in_specs=in_specs,
        out_specs=[pl.BlockSpec((None, tb, D_B), lambda b, i: (b, i, 0)), st_spec],
        out_shape=[jax.ShapeDtypeStruct((B, L, D_B), BF16),
                   jax.ShapeDtypeStruct((B, N_HEADS_B, HEAD_K_B, HEAD_V_B), F32)],
        scratch_shapes=[pltpu.VMEM((N_HEADS_B, HEAD_V_B, HEAD_K_B), F32),
                        pltpu.VMEM((n_lev, tb, D_B), BF16),
                        act(), act(), act(), act(), act(),
                        pltpu.VMEM((tb, D_B), F32)],
        compiler_params=pltpu.CompilerParams(dimension_semantics=("arbitrary", "arbitrary"),
                                             vmem_limit_bytes=VMEM_LIMIT),
        name="hgrn",
    )(*args)


def _gelu_tanh(x):
    return 0.5 * x * (1.0 + jnp.tanh(0.7978845608028654 * (x + 0.044715 * (x * x * x))))


def _mlp_front(x, oa, ob, zab, wa_ref, wb_ref, wo_ref, g2_ref, wg_ref, wu_ref):
    ma = _dot(oa, wa_ref[...])
    mb = _dot(ob, wb_ref[...])
    za = zab[:, 0:D_MODEL].astype(F32)
    zb = zab[:, D_MODEL:2 * D_MODEL].astype(F32)
    merged = _sigmoid(za) * ma + _sigmoid(zb) * mb
    x1 = x + _dot(merged.astype(BF16), wo_ref[...])
    h2 = _rms(x1, g2_ref[...]).astype(BF16)
    return x1, _dot(h2, wg_ref[...]), _dot(h2, wu_ref[...])


def _mlp_back(x1, ac, u, wd_ref, gf_ref):
    g = (_gelu_tanh(ac) * u).astype(BF16)
    return _rms(x1 + _dot(g, wd_ref[...]), gf_ref[...])


def _mlp_kernel(*refs, tm, sub, seg, has_state):
    if has_state:
        (x_ref, oa_ref, ob_ref, zab_ref, cst_ref, wa_ref, wb_ref, wo_ref, g2_ref, wg_ref, wu_ref, cw_ref, cb_ref,
         wd_ref, gf_ref, y_ref, cnew_ref, a_scr) = refs
    else:
        (x_ref, oa_ref, ob_ref, zab_ref, wa_ref, wb_ref, wo_ref, g2_ref, wg_ref, wu_ref, cw_ref, cb_ref,
         wd_ref, gf_ref, y_ref, cnew_ref, a_scr) = refs
    cw0, cw1, cw2 = cw_ref[0:1, :], cw_ref[1:2, :], cw_ref[2:3, :]
    cb = cb_ref[...]
    if not has_state:
        t = pl.program_id(1)

        @pl.when(t == 0)
        def _():
            a_scr[0:8, :] = jnp.zeros((8, D_FF), F32)

    def conv(slab, base, a_rows):
        n = a_rows.shape[0]
        slab[base + 8:base + 8 + n, :] = a_rows
        return cb + cw0 * slab[base + 6:base + 6 + n, :] + cw1 * slab[base + 7:base + 7 + n, :] + cw2 * a_rows

    for j in range(tm // sub):
        rows = slice(j * sub, (j + 1) * sub)
        x1, a, u = _mlp_front(x_ref[rows, :], oa_ref[rows, :], ob_ref[rows, :], zab_ref[rows, :],
                              wa_ref, wb_ref, wo_ref, g2_ref, wg_ref, wu_ref)
        if has_state:
            parts = []
            for s in range(j * sub // seg, (j + 1) * sub // seg):
                slab = a_scr.at[s]
                slab[6:8, :] = cst_ref[s]
                parts.append(conv(slab, 0, a[s * seg - j * sub:(s + 1) * seg - j * sub, :]))
                cnew_ref[s] = slab[6 + seg:8 + seg, :]
            ac = jnp.concatenate(parts, axis=0)
        else:
            ac = conv(a_scr, j * sub, a)
        y_ref[rows, :] = _mlp_back(x1, ac, u, wd_ref, gf_ref)

    if not has_state:
        a_scr[0:8, :] = a_scr[tm:tm + 8, :]

        @pl.when(t == pl.num_programs(1) - 1)
        def _():
            cnew_ref[...] = a_scr[6:8, :]


def _mlp(x, oa, ob, zab, conv_state, wa, wb, wo, g2, wg, wu, cw, cb, wd, gf, *, tm, sub):
    B, L, _ = x.shape
    has_state = conv_state is not None
    row = lambda w: pl.BlockSpec((None, tm, w), lambda b, t: (b, t, 0))
    in_specs = [row(D_MODEL), row(D_A), row(D_B), row(2 * D_MODEL)]
    args = [x, oa, ob, zab]
    if has_state:
        nseg = conv_state.shape[0]
        seg = L // nseg
        assert tm == L and B == 1
        in_specs.append(_const_spec(conv_state.shape))
        args.append(conv_state)
        cnew_shape = conv_state.shape
        cnew_spec = pl.BlockSpec(cnew_shape, lambda b, t: (0, 0, 0))
        scratch = pltpu.VMEM((nseg, seg + 8, D_FF), F32)
    else:
        seg = tm
        cnew_shape = (B, CONV_W - 1, D_FF)
        cnew_spec = pl.BlockSpec((None, CONV_W - 1, D_FF), lambda b, t: (b, 0, 0))
        scratch = pltpu.VMEM((tm + 8, D_FF), F32)
    weights = [wa, wb, wo, g2.reshape(1, D_MODEL), wg, wu, cw, cb.reshape(1, D_FF), wd, gf.reshape(1, D_MODEL)]
    in_specs += [_const_spec(w.shape) for w in weights]
    return pl.pallas_call(
        functools.partial(_mlp_kernel, tm=tm, sub=sub, seg=seg, has_state=has_state),
        grid=(B, L // tm),
        in_specs=in_specs,
        out_specs=[row(D_MODEL), cnew_spec],
        out_shape=[jax.ShapeDtypeStruct((B, L, D_MODEL), F32), jax.ShapeDtypeStruct(cnew_shape, F32)],
        scratch_shapes=[scratch],
        compiler_params=pltpu.CompilerParams(dimension_semantics=("arbitrary", "arbitrary"),
                                             vmem_limit_bytes=VMEM_LIMIT),
        name="mlp",
    )(*args, *weights)


def _heads(t, n_heads):
    B, L, D = t.shape
    return t.reshape(B, L, n_heads, D // n_heads).transpose(0, 2, 1, 3)


def kernel(x_prompt, x_sample, cache_attn_k, cache_attn_v, state_hgrn, state_ffn_conv, norm_mix_g, w_in, rel_bias,
           hgrn_lb_logits, hgrn_norm_g, w_branch_a, w_branch_b, w_out, norm_ffn_g, w_ffn_gate, w_ffn_up,
           ffn_conv_w, ffn_conv_b, w_ffn_down, norm_final_g):
    depth = w_in.shape[0]
    assert depth == 1, "single-layer step"
    Bp, Lp, _ = x_prompt.shape
    Bs, Ls, _ = x_sample.shape
    w_in_bf = w_in[0].astype(BF16)
    wa, wb, wo = w_branch_a[0].astype(BF16), w_branch_b[0].astype(BF16), w_out[0].astype(BF16)
    wg, wu, wd = w_ffn_gate[0].astype(BF16), w_ffn_up[0].astype(BF16), w_ffn_down[0].astype(BF16)
    mlp_w = (wa, wb, wo, norm_ffn_g[0], wg, wu, ffn_conv_w[0], ffn_conv_b[0], wd, norm_final_g)

    keep = min(ATT_WINDOW, Lp)
    q, k, v, hg, zab, kvf = _inproj(x_prompt, norm_mix_g[0], w_in_bf, tm=512, keep_rows=keep)
    oa = _attn_prompt(rel_bias[0], q, k, v, qb=512)
    ob, s_p = _hgrn(hg, hgrn_lb_logits, hgrn_norm_g[0], None, tb=512, chunk=CHUNK)
    y_p, conv_p = _mlp(x_prompt, oa, ob, zab, None, *mlp_w, tm=512, sub=256)
    k_p = _heads(kvf[:, :, :D_A], N_HEADS_A)
    v_p = _heads(kvf[:, :, D_A:], N_HEADS_A)

    n_s = Bs * Ls
    xs = x_sample.reshape(1, n_s, D_MODEL)
    q, k, v, hg, zab, kvf = _inproj(xs, norm_mix_g[0], w_in_bf, tm=n_s, keep_rows=n_s)
    w_att = cache_attn_k.shape[3]
    assert w_att == ATT_WINDOW and Ls <= CHUNK
    pad_q = jnp.zeros((Bs, CHUNK - Ls, D_A), BF16)

    def window(cache, new):
        past = cache[0].astype(BF16).transpose(0, 2, 1, 3).reshape(Bs, w_att, D_A)
        return jnp.concatenate([jnp.zeros((Bs, CHUNK, D_A), BF16), past, new.reshape(Bs, Ls, D_A), pad_q], axis=1)

    qwin = jnp.concatenate([q.reshape(Bs, Ls, D_A), pad_q], axis=1)
    oa = _attn_window(rel_bias[0], qwin, window(cache_attn_k, k), window(cache_attn_v, v),
                      n_valid=CHUNK + w_att + Ls)[:, :Ls].reshape(1, n_s, D_A)
    ob, s_s = _hgrn(hg.reshape(Bs, Ls, 4 * D_B), hgrn_lb_logits, hgrn_norm_g[0], state_hgrn[0], tb=Ls, chunk=Ls)
    y_s, conv_s = _mlp(xs, oa, ob.reshape(1, n_s, D_B), zab, state_ffn_conv[0], *mlp_w, tm=n_s, sub=n_s)
    kvf = kvf.reshape(Bs, Ls, 2 * D_A)
    k_s = _heads(kvf[:, :, :D_A], N_HEADS_A)
    v_s = _heads(kvf[:, :, D_A:], N_HEADS_A)

    return (y_p, y_s.reshape(Bs, Ls, D_MODEL),
            k_p[None], v_p[None], s_p[None], conv_p[None],
            k_s[None], v_s[None], s_s[None], conv_s[None])
```

```python
import functools

import jax
import jax.numpy as jnp
from jax import lax
from jax.experimental import pallas as pl
from jax.experimental.pallas import tpu as pltpu

F32 = jnp.float32
BF16 = jnp.bfloat16

D_MODEL = 1024
CHUNK = 64
N_PREV_CHUNKS = 8
ATT_WINDOW = N_PREV_CHUNKS * CHUNK
N_HEADS_A = 8
HEAD_DIM_A = 64
D_A = N_HEADS_A * HEAD_DIM_A
REL_CLIP = 128
N_REL = CHUNK + REL_CLIP
N_HEADS_B = 4
HEAD_K_B = 128
HEAD_V_B = 128
D_B = N_HEADS_B * HEAD_K_B
D_FF = 2816
CONV_W = 3
EPS = 1e-6
D_IN = 3 * D_A + 4 * D_B + 2 * D_MODEL
ATT_SCALE = HEAD_DIM_A ** -0.5

WIN = (N_PREV_CHUNKS + 2) * CHUNK
HEADS_PER_GROUP = 4
GROUP_LANES = HEADS_PER_GROUP * HEAD_DIM_A
NEG = -1e30

VMEM_LIMIT = 56 * 1024 * 1024


def _sigmoid(x):
    return 1.0 / (1.0 + jnp.exp(-x))


def _rms(x, g):
    return x * lax.rsqrt(jnp.mean(x * x, axis=-1, keepdims=True) + EPS) * g


def _dot(a, b):
    return jnp.dot(a, b, preferred_element_type=F32)


def _dot_nt(a, b):
    return lax.dot_general(a, b, (((1,), (1,)), ((), ())), preferred_element_type=F32)


def _dot_tn(a, b):
    return lax.dot_general(a, b, (((0,), (0,)), ((), ())), preferred_element_type=F32)


def _const_spec(shape):
    nd = len(shape)
    return pl.BlockSpec(shape, lambda *_: (0,) * nd, pipeline_mode=pl.Buffered(1))


def _inproj_kernel(x_ref, g_ref, w_ref, q_ref, k_ref, v_ref, hg_ref, zab_ref, kvf_ref, *,
                   tiles_per_batch, tail_tiles):
    h = _rms(x_ref[...], g_ref[...]).astype(BF16)

    def proj(c0, c1):
        return _dot(h, w_ref[:, c0:c1])

    q_ref[...] = (proj(0, D_A) * ATT_SCALE).astype(BF16)
    kf = proj(D_A, 2 * D_A)
    vf = proj(2 * D_A, 3 * D_A)
    k_ref[...] = kf.astype(BF16)
    v_ref[...] = vf.astype(BF16)
    off = 3 * D_A
    for j in range(4):
        hg_ref[:, j * D_B:(j + 1) * D_B] = proj(off + j * D_B, off + (j + 1) * D_B)
    off = 3 * D_A + 4 * D_B
    for j in range(4):
        zab_ref[:, j * 512:(j + 1) * 512] = proj(off + j * 512, off + (j + 1) * 512).astype(BF16)

    def write_kv():
        kvf_ref[:, 0:D_A] = kf
        kvf_ref[:, D_A:2 * D_A] = vf

    if tail_tiles >= tiles_per_batch:
        write_kv()
    else:
        pl.when(pl.program_id(1) >= tiles_per_batch - tail_tiles)(write_kv)


def _inproj(x, g, w_bf, *, tm, keep_rows):
    B, L, _ = x.shape
    T = L // tm
    tail = keep_rows // tm
    assert T * tm == L and tail * tm == keep_rows and tail >= 1
    row = lambda w: pl.BlockSpec((None, tm, w), lambda b, t: (b, t, 0))
    kern = functools.partial(_inproj_kernel, tiles_per_batch=T, tail_tiles=tail)
    return pl.pallas_call(
        kern,
        grid=(B, T),
        in_specs=[row(D_MODEL), _const_spec((1, D_MODEL)), _const_spec((D_MODEL, D_IN))],
        out_specs=[row(D_A), row(D_A), row(D_A), row(4 * D_B), row(2 * D_MODEL),
                   pl.BlockSpec((None, tm, 2 * D_A), lambda b, t: (b, jnp.maximum(t - (T - tail), 0), 0))],
        out_shape=[jax.ShapeDtypeStruct((B, L, D_A), BF16)] * 3
        + [jax.ShapeDtypeStruct((B, L, 4 * D_B), F32), jax.ShapeDtypeStruct((B, L, 2 * D_MODEL), BF16),
           jax.ShapeDtypeStruct((B, keep_rows, 2 * D_A), F32)],
        compiler_params=pltpu.CompilerParams(dimension_semantics=("arbitrary", "arbitrary"),
                                             vmem_limit_bytes=VMEM_LIMIT),
        name="inproj",
    )(x, g.reshape(1, D_MODEL), w_bf)


def _build_bias(relb_ref, bias_scr):
    col = lax.broadcasted_iota(jnp.int32, (N_HEADS_A, WIN), 1)
    tab = relb_ref[...]
    base = jnp.broadcast_to(tab[:, N_REL - 1:N_REL], (N_HEADS_A, WIN))
    for t in range(N_REL - 1):
        base = jnp.where(col == (WIN - 1 - t), tab[:, t:t + 1], base)
    row = lax.broadcasted_iota(jnp.int32, (CHUNK, WIN), 0)
    colq = lax.broadcasted_iota(jnp.int32, (CHUNK, WIN), 1)
    for h in range(N_HEADS_A):
        x = jnp.broadcast_to(base[h:h + 1, :], (CHUNK, WIN))
        shift = 1
        while shift < CHUNK:
            x = jnp.where((row & shift) != 0, pltpu.roll(x, shift, 1), x)
            shift *= 2
        bias_scr[h] = jnp.where(colq >= CHUNK, x, NEG)


def _attend_chunk(qc, kw, vw, bias_scr, mrow):
    lane_head = lax.broadcasted_iota(jnp.int32, (CHUNK, GROUP_LANES), 1) // HEAD_DIM_A
    outs = []
    for g in range(N_HEADS_A // HEADS_PER_GROUP):
        ls = slice(g * GROUP_LANES, (g + 1) * GROUP_LANES)
        q256 = qc[:, ls]
        qs = jnp.concatenate(
            [jnp.where(lane_head == hh, q256, jnp.zeros_like(q256)) for hh in range(HEADS_PER_GROUP)], axis=0)
        s = _dot_nt(qs, kw[:, ls])
        bias = jnp.concatenate([bias_scr[g * HEADS_PER_GROUP + hh] for hh in range(HEADS_PER_GROUP)], axis=0)
        s = s + bias + mrow
        m = jnp.max(s, axis=-1, keepdims=True)
        p = jnp.exp(s - m)
        l = jnp.sum(p, axis=-1, keepdims=True)
        pv = _dot(p.astype(BF16), vw[:, ls]) * (1.0 / l)
        o = jnp.zeros((CHUNK, GROUP_LANES), F32)
        for hh in range(HEADS_PER_GROUP):
            o = jnp.where(lane_head == hh, pv[hh * CHUNK:(hh + 1) * CHUNK, :], o)
        outs.append(o)
    return jnp.concatenate(outs, axis=1)


def _attn_prompt_kernel(relb_ref, q_ref, k_ref, v_ref, o_ref, kbuf, vbuf, bias_scr, *, qb):
    i = pl.program_id(1)
    hist = WIN - CHUNK

    @pl.when((pl.program_id(0) == 0) & (i == 0))
    def _():
        _build_bias(relb_ref, bias_scr)

    @pl.when(i == 0)
    def _():
        kbuf[0:hist, :] = jnp.zeros((hist, D_A), BF16)
        vbuf[0:hist, :] = jnp.zeros((hist, D_A), BF16)

    @pl.when(i > 0)
    def _():
        kbuf[0:hist, :] = kbuf[qb:qb + hist, :]
        vbuf[0:hist, :] = vbuf[qb:qb + hist, :]

    kbuf[hist:hist + qb, :] = k_ref[...]
    vbuf[hist:hist + qb, :] = v_ref[...]
    col = lax.broadcasted_iota(jnp.int32, (1, WIN), 1)

    for cl in range(qb // CHUNK):
        r0 = cl * CHUNK
        lo = jnp.where(i == 0, hist - r0, 0)
        mrow = jnp.where(col >= lo, 0.0, NEG).astype(F32)
        o = _attend_chunk(q_ref[r0:r0 + CHUNK, :], kbuf[r0:r0 + WIN, :], vbuf[r0:r0 + WIN, :], bias_scr, mrow)
        o_ref[r0:r0 + CHUNK, :] = o.astype(BF16)


def _attn_prompt(relb, q, k, v, *, qb):
    B, L, _ = q.shape
    blk = pl.BlockSpec((None, qb, D_A), lambda b, i: (b, i, 0))
    return pl.pallas_call(
        functools.partial(_attn_prompt_kernel, qb=qb),
        grid=(B, L // qb),
        in_specs=[_const_spec((N_HEADS_A, N_REL)), blk, blk, blk],
        out_specs=blk,
        out_shape=jax.ShapeDtypeStruct((B, L, D_A), BF16),
        scratch_shapes=[pltpu.VMEM((WIN - CHUNK + qb, D_A), BF16), pltpu.VMEM((WIN - CHUNK + qb, D_A), BF16),
                        pltpu.VMEM((N_HEADS_A, CHUNK, WIN), F32)],
        compiler_params=pltpu.CompilerParams(dimension_semantics=("arbitrary", "arbitrary"),
                                             vmem_limit_bytes=VMEM_LIMIT),
        name="attn_prompt",
    )(relb, q, k, v)


def _attn_window_kernel(relb_ref, q_ref, k_ref, v_ref, o_ref, bias_scr, *, n_valid):
    @pl.when(pl.program_id(0) == 0)
    def _():
        _build_bias(relb_ref, bias_scr)

    col = lax.broadcasted_iota(jnp.int32, (1, WIN), 1)
    mrow = jnp.where(col < n_valid, 0.0, NEG).astype(F32)
    o_ref[...] = _attend_chunk(q_ref[...], k_ref[...], v_ref[...], bias_scr, mrow).astype(BF16)


def _attn_window(relb, qwin, kwin, vwin, *, n_valid):
    B = qwin.shape[0]
    return pl.pallas_call(
        functools.partial(_attn_window_kernel, n_valid=n_valid),
        grid=(B,),
        in_specs=[_const_spec((N_HEADS_A, N_REL)),
                  pl.BlockSpec((None, CHUNK, D_A), lambda b: (b, 0, 0)),
                  pl.BlockSpec((None, WIN, D_A), lambda b: (b, 0, 0)),
                  pl.BlockSpec((None, WIN, D_A), lambda b: (b, 0, 0))],
        out_specs=pl.BlockSpec((None, CHUNK, D_A), lambda b: (b, 0, 0)),
        out_shape=jax.ShapeDtypeStruct((B, CHUNK, D_A), BF16),
        scratch_shapes=[pltpu.VMEM((N_HEADS_A, CHUNK, WIN), F32)],
        compiler_params=pltpu.CompilerParams(dimension_semantics=("arbitrary",), vmem_limit_bytes=VMEM_LIMIT),
        name="attn_window",
    )(relb, qwin, kwin, vwin)


def _block_bcast(x, period, src):
    R, C = x.shape
    if period % 8 == 0:
        x3 = x.reshape(R // period, period, C)
        return jnp.broadcast_to(x3[:, src:src + 1, :], x3.shape).reshape(R, C)
    off = lax.broadcasted_iota(jnp.int32, (R, C), 0) % period
    out = x
    for j in range(period):
        if j != src:
            out = jnp.where(off == j, pltpu.roll(x, (j - src) % R, 0), out)
    return out


def _hgrn_kernel(*refs, tb, chunk, has_s0):
    if has_s0:
        hg_ref, lbl_ref, gn_ref, s0_ref, ob_ref, sout_ref, st_scr = refs
    else:
        hg_ref, lbl_ref, gn_ref, ob_ref, sout_ref, st_scr = refs
    i = pl.program_id(1)

    @pl.when(i == 0)
    def _():
        for h in range(N_HEADS_B):
            if has_s0:
                st_scr[h] = s0_ref[h].T
            else:
                st_scr[h] = jnp.zeros((HEAD_V_B, HEAD_K_B), F32)

    lg = lbl_ref[...]
    e = jnp.exp(lg - jnp.max(lg, axis=0, keepdims=True))
    lb = e[0:1, :] / jnp.sum(e, axis=0, keepdims=True)

    levels = []
    m = chunk // 2
    while m >= 1:
        levels.append(m)
        m //= 2
    rc = lax.broadcasted_iota(jnp.int32, (chunk, D_B), 0)
    lefts = [(rc % (2 * m)) < m for m in levels]
    signs = [jnp.where(left, 1.0, -1.0).astype(F32) for left in lefts]
    t_id = lax.broadcasted_iota(jnp.int32, (chunk, chunk), 0)
    s_id = lax.broadcasted_iota(jnp.int32, (chunk, chunk), 1)
    diff = t_id ^ s_id
    on_diag = diff == 0
    lev_masks = [jnp.where(t_id > s_id, diff, 0) // m == 1 for m in levels]
    gn = gn_ref[...]

    for c in range(tb // chunk):
        rows = slice(c * chunk, (c + 1) * chunk)
        qb = hg_ref[rows, 0:D_B]
        f = lb + (1.0 - lb) * _sigmoid(hg_ref[rows, D_B:2 * D_B])
        q = qb * _sigmoid(qb)
        k = 1.0 - f
        b = jnp.log(f)
        s = 1
        while s < chunk:
            b = b + jnp.where(rc >= s, pltpu.roll(b, s, 0), 0.0)
            s *= 2
        b_end = b[chunk - 1:chunk, :]
        q_bf = q.astype(BF16)
        k_bf = k.astype(BF16)
        qi = (q * jnp.exp(b)).astype(BF16)
        ks = (k * jnp.exp(b_end - b)).astype(BF16)
        v = hg_ref[rows, 2 * D_B:3 * D_B].astype(BF16)
        dec = jnp.exp(b_end)
        zs = []
        for li, m in enumerate(levels):
            edge = _block_bcast(b, 2 * m, m - 1)
            w = jnp.exp((edge - b) * signs[li])
            zs.append((jnp.where(lefts[li], k, q) * w).astype(BF16))
        for h in range(N_HEADS_B):
            ls = slice(h * HEAD_K_B, (h + 1) * HEAD_K_B)
            a = jnp.where(on_diag, _dot_nt(q_bf[:, ls], k_bf[:, ls]), 0.0)
            for li in range(len(levels)):
                a = jnp.where(lev_masks[li], _dot_nt(zs[li][:, ls], zs[li][:, ls]), a)
            st = st_scr[h]
            o = _dot_nt(qi[:, ls], st.astype(BF16)) + _dot(a.astype(BF16), v[:, ls])
            st_scr[h] = st * dec[:, ls] + _dot_tn(v[:, ls], ks[:, ls])
            on = _rms(o, gn)
            gb = hg_ref[rows, 3 * D_B + h * HEAD_V_B:3 * D_B + (h + 1) * HEAD_V_B]
            ob_ref[rows, ls] = (on * (gb * _sigmoid(gb))).astype(BF16)

    @pl.when(i == pl.num_programs(1) - 1)
    def _():
        for h in range(N_HEADS_B):
            sout_ref[h] = st_scr[h].T


def _hgrn(hg, lb_logits, g_norm, s0, *, tb, chunk):
    B, L, _ = hg.shape
    has_s0 = s0 is not None
    st_spec = pl.BlockSpec((None, N_HEADS_B, HEAD_K_B, HEAD_V_B), lambda b, i: (b, 0, 0, 0))
    in_specs = [pl.BlockSpec((None, tb, 4 * D_B), lambda b, i: (b, i, 0)),
                _const_spec(lb_logits.shape), _const_spec((1, HEAD_V_B))]
    args = [hg, lb_logits, g_norm.reshape(1, HEAD_V_B)]
    if has_s0:
        in_specs.append(st_spec)
        args.append(s0)
    return pl.pallas_call(
        functools.partial(_hgrn_kernel, tb=tb, chunk=chunk, has_s0=has_s0),
        grid=(B, L // tb),
        in_specs=in_specs,
        out_specs=[pl.BlockSpec((None, tb, D_B), lambda b, i: (b, i, 0)), st_spec],
        out_shape=[jax.ShapeDtypeStruct((B, L, D_B), BF16),
                   jax.ShapeDtypeStruct((B, N_HEADS_B, HEAD_K_B, HEAD_V_B), F32)],
        scratch_shapes=[pltpu.VMEM((N_HEADS_B, HEAD_V_B, HEAD_K_B), F32)],
        compiler_params=pltpu.CompilerParams(dimension_semantics=("arbitrary", "arbitrary"),
                                             vmem_limit_bytes=VMEM_LIMIT),
        name="hgrn",
    )(*args)


def _gelu_tanh(x):
    return 0.5 * x * (1.0 + jnp.tanh(0.7978845608028654 * (x + 0.044715 * (x * x * x))))


def _mlp_front(x, oa, ob, zab, wa_ref, wb_ref, wo_ref, g2_ref, wg_ref, wu_ref):
    ma = _dot(oa, wa_ref[...])
    mb = _dot(ob, wb_ref[...])
    za = zab[:, 0:D_MODEL].astype(F32)
    zb = zab[:, D_MODEL:2 * D_MODEL].astype(F32)
    merged = _sigmoid(za) * ma + _sigmoid(zb) * mb
    x1 = x + _dot(merged.astype(BF16), wo_ref[...])
    h2 = _rms(x1, g2_ref[...]).astype(BF16)
    return x1, _dot(h2, wg_ref[...]), _dot(h2, wu_ref[...])


def _mlp_back(x1, ac, u, wd_ref, gf_ref):
    g = (_gelu_tanh(ac) * u).astype(BF16)
    return _rms(x1 + _dot(g, wd_ref[...]), gf_ref[...])


def _mlp_kernel(*refs, tm, sub, seg, has_state):
    if has_state:
        (x_ref, oa_ref, ob_ref, zab_ref, cst_ref, wa_ref, wb_ref, wo_ref, g2_ref, wg_ref, wu_ref, cw_ref, cb_ref,
         wd_ref, gf_ref, y_ref, cnew_ref, a_scr) = refs
    else:
        (x_ref, oa_ref, ob_ref, zab_ref, wa_ref, wb_ref, wo_ref, g2_ref, wg_ref, wu_ref, cw_ref, cb_ref,
         wd_ref, gf_ref, y_ref, cnew_ref, a_scr) = refs
    cw0, cw1, cw2 = cw_ref[0:1, :], cw_ref[1:2, :], cw_ref[2:3, :]
    cb = cb_ref[...]
    if not has_state:
        t = pl.program_id(1)

        @pl.when(t == 0)
        def _():
            a_scr[0:8, :] = jnp.zeros((8, D_FF), F32)

    def conv(slab, base, a_rows):
        n = a_rows.shape[0]
        slab[base + 8:base + 8 + n, :] = a_rows
        return cb + cw0 * slab[base + 6:base + 6 + n, :] + cw1 * slab[base + 7:base + 7 + n, :] + cw2 * a_rows

    for j in range(tm // sub):
        rows = slice(j * sub, (j + 1) * sub)
        x1, a, u = _mlp_front(x_ref[rows, :], oa_ref[rows, :], ob_ref[rows, :], zab_ref[rows, :],
                              wa_ref, wb_ref, wo_ref, g2_ref, wg_ref, wu_ref)
        if has_state:
            parts = []
            for s in range(j * sub // seg, (j + 1) * sub // seg):
                slab = a_scr.at[s]
                slab[6:8, :] = cst_ref[s]
                parts.append(conv(slab, 0, a[s * seg - j * sub:(s + 1) * seg - j * sub, :]))
                cnew_ref[s] = slab[6 + seg:8 + seg, :]
            ac = jnp.concatenate(parts, axis=0)
        else:
            ac = conv(a_scr, j * sub, a)
        y_ref[rows, :] = _mlp_back(x1, ac, u, wd_ref, gf_ref)

    if not has_state:
        a_scr[0:8, :] = a_scr[tm:tm + 8, :]

        @pl.when(t == pl.num_programs(1) - 1)
        def _():
            cnew_ref[...] = a_scr[6:8, :]


def _mlp(x, oa, ob, zab, conv_state, wa, wb, wo, g2, wg, wu, cw, cb, wd, gf, *, tm, sub):
    B, L, _ = x.shape
    has_state = conv_state is not None
    row = lambda w: pl.BlockSpec((None, tm, w), lambda b, t: (b, t, 0))
    in_specs = [row(D_MODEL), row(D_A), row(D_B), row(2 * D_MODEL)]
    args = [x, oa, ob, zab]
    if has_state:
        nseg = conv_state.shape[0]
        seg = L // nseg
        assert tm == L and B == 1
        in_specs.append(_const_spec(conv_state.shape))
        args.append(conv_state)
        cnew_shape = conv_state.shape
        cnew_spec = pl.BlockSpec(cnew_shape, lambda b, t: (0, 0, 0))
        scratch = pltpu.VMEM((nseg, seg + 8, D_FF), F32)
    else:
        seg = tm
        cnew_shape = (B, CONV_W - 1, D_FF)
        cnew_spec = pl.BlockSpec((None, CONV_W - 1, D_FF), lambda b, t: (b, 0, 0))
        scratch = pltpu.VMEM((tm + 8, D_FF), F32)
    weights = [wa, wb, wo, g2.reshape(1, D_MODEL), wg, wu, cw, cb.reshape(1, D_FF), wd, gf.reshape(1, D_MODEL)]
    in_specs += [_const_spec(w.shape) for w in weights]
    return pl.pallas_call(
        functools.partial(_mlp_kernel, tm=tm, sub=sub, seg=seg, has_state=has_state),
        grid=(B, L // tm),
        in_specs=in_specs,
        out_specs=[row(D_MODEL), cnew_spec],
        out_shape=[jax.ShapeDtypeStruct((B, L, D_MODEL), F32), jax.ShapeDtypeStruct(cnew_shape, F32)],
        scratch_shapes=[scratch],
        compiler_params=pltpu.CompilerParams(dimension_semantics=("arbitrary", "arbitrary"),
                                             vmem_limit_bytes=VMEM_LIMIT),
        name="mlp",
    )(*args, *weights)


def _heads(t, n_heads):
    B, L, D = t.shape
    return t.reshape(B, L, n_heads, D // n_heads).transpose(0, 2, 1, 3)


def kernel(x_prompt, x_sample, cache_attn_k, cache_attn_v, state_hgrn, state_ffn_conv, norm_mix_g, w_in, rel_bias,
           hgrn_lb_logits, hgrn_norm_g, w_branch_a, w_branch_b, w_out, norm_ffn_g, w_ffn_gate, w_ffn_up,
           ffn_conv_w, ffn_conv_b, w_ffn_down, norm_final_g):
    depth = w_in.shape[0]
    assert depth == 1, "single-layer step"
    Bp, Lp, _ = x_prompt.shape
    Bs, Ls, _ = x_sample.shape
    w_in_bf = w_in[0].astype(BF16)
    wa, wb, wo = w_branch_a[0].astype(BF16), w_branch_b[0].astype(BF16), w_out[0].astype(BF16)
    wg, wu, wd = w_ffn_gate[0].astype(BF16), w_ffn_up[0].astype(BF16), w_ffn_down[0].astype(BF16)
    mlp_w = (wa, wb, wo, norm_ffn_g[0], wg, wu, ffn_conv_w[0], ffn_conv_b[0], wd, norm_final_g)

    keep = min(ATT_WINDOW, Lp)
    q, k, v, hg, zab, kvf = _inproj(x_prompt, norm_mix_g[0], w_in_bf, tm=512, keep_rows=keep)
    oa = _attn_prompt(rel_bias[0], q, k, v, qb=512)
    ob, s_p = _hgrn(hg, hgrn_lb_logits, hgrn_norm_g[0], None, tb=512, chunk=CHUNK)
    y_p, conv_p = _mlp(x_prompt, oa, ob, zab, None, *mlp_w, tm=512, sub=256)
    k_p = _heads(kvf[:, :, :D_A], N_HEADS_A)
    v_p = _heads(kvf[:, :, D_A:], N_HEADS_A)

    n_s = Bs * Ls
    xs = x_sample.reshape(1, n_s, D_MODEL)
    q, k, v, hg, zab, kvf = _inproj(xs, norm_mix_g[0], w_in_bf, tm=n_s, keep_rows=n_s)
    w_att = cache_attn_k.shape[3]
    assert w_att == ATT_WINDOW and Ls <= CHUNK
    pad_q = jnp.zeros((Bs, CHUNK - Ls, D_A), BF16)

    def window(cache, new):
        past = cache[0].astype(BF16).transpose(0, 2, 1, 3).reshape(Bs, w_att, D_A)
        return jnp.concatenate([jnp.zeros((Bs, CHUNK, D_A), BF16), past, new.reshape(Bs, Ls, D_A), pad_q], axis=1)

    qwin = jnp.concatenate([q.reshape(Bs, Ls, D_A), pad_q], axis=1)
    oa = _attn_window(rel_bias[0], qwin, window(cache_attn_k, k), window(cache_attn_v, v),
                      n_valid=CHUNK + w_att + Ls)[:, :Ls].reshape(1, n_s, D_A)
    ob, s_s = _hgrn(hg.reshape(Bs, Ls, 4 * D_B), hgrn_lb_logits, hgrn_norm_g[0], state_hgrn[0], tb=Ls, chunk=Ls)
    y_s, conv_s = _mlp(xs, oa, ob.reshape(1, n_s, D_B), zab, state_ffn_conv[0], *mlp_w, tm=n_s, sub=n_s)
    kvf = kvf.reshape(Bs, Ls, 2 * D_A)
    k_s = _heads(kvf[:, :, :D_A], N_HEADS_A)
    v_s = _heads(kvf[:, :, D_A:], N_HEADS_A)

    return (y_p, y_s.reshape(Bs, Ls, D_MODEL),
            k_p[None], v_p[None], s_p[None], conv_p[None],
            k_s[None], v_s[None], s_s[None], conv_s[None])
```

```python
import functools

import jax
import jax.numpy as jnp
from jax import lax
from jax.experimental import pallas as pl
from jax.experimental.pallas import tpu as pltpu

F32 = jnp.float32
BF16 = jnp.bfloat16

D_MODEL = 1024
CHUNK = 64
N_PREV_CHUNKS = 8
ATT_WINDOW = N_PREV_CHUNKS * CHUNK
N_HEADS_A = 8
HEAD_DIM_A = 64
D_A = N_HEADS_A * HEAD_DIM_A
REL_CLIP = 128
N_REL = CHUNK + REL_CLIP
N_HEADS_B = 4
HEAD_K_B = 128
HEAD_V_B = 128
D_B = N_HEADS_B * HEAD_K_B
D_FF = 2816
CONV_W = 3
EPS = 1e-6
D_IN = 3 * D_A + 4 * D_B + 2 * D_MODEL
ATT_SCALE = HEAD_DIM_A ** -0.5

PAIR = 2 * CHUNK
WIN = (N_PREV_CHUNKS + 2) * CHUNK
LOG2E = 1.4426950408889634
HEADS_PER_GROUP = 4
GROUP_LANES = HEADS_PER_GROUP * HEAD_DIM_A
NEG = -1e30

VMEM_LIMIT = 56 * 1024 * 1024


def _sigmoid(x):
    return 1.0 / (1.0 + jnp.exp2(x * -LOG2E))


def _sigmoid_gate(x):
    return pl.reciprocal(1.0 + jnp.exp2(x * -LOG2E), approx=True)


def _rms(x, g):
    return x * lax.rsqrt(jnp.mean(x * x, axis=-1, keepdims=True) + EPS) * g


def _dot(a, b):
    return jnp.dot(a, b, preferred_element_type=F32)


def _dot_nt(a, b):
    return lax.dot_general(a, b, (((1,), (1,)), ((), ())), preferred_element_type=F32)


def _dot_tn(a, b):
    return lax.dot_general(a, b, (((0,), (0,)), ((), ())), preferred_element_type=F32)


def _const_spec(shape):
    nd = len(shape)
    return pl.BlockSpec(shape, lambda *_: (0,) * nd, pipeline_mode=pl.Buffered(1))


def _inproj_kernel(x_ref, g_ref, w_ref, q_ref, k_ref, v_ref, hg_ref, zab_ref, kvf_ref, *,
                   tiles_per_batch, tail_tiles):
    h = _rms(x_ref[...], g_ref[...]).astype(BF16)

    def proj(c0, c1):
        return _dot(h, w_ref[:, c0:c1])

    q_ref[...] = (proj(0, D_A) * (ATT_SCALE * LOG2E)).astype(BF16)
    kf = proj(D_A, 2 * D_A)
    vf = proj(2 * D_A, 3 * D_A)
    k_ref[...] = kf.astype(BF16)
    v_ref[...] = vf.astype(BF16)
    off = 3 * D_A
    for j in range(4):
        hg_ref[:, j * D_B:(j + 1) * D_B] = proj(off + j * D_B, off + (j + 1) * D_B)
    off = 3 * D_A + 4 * D_B
    for j in range(4):
        zab_ref[:, j * 512:(j + 1) * 512] = proj(off + j * 512, off + (j + 1) * 512).astype(BF16)

    def write_kv():
        kvf_ref[:, 0:D_A] = kf
        kvf_ref[:, D_A:2 * D_A] = vf

    if tail_tiles >= tiles_per_batch:
        write_kv()
    else:
        pl.when(pl.program_id(1) >= tiles_per_batch - tail_tiles)(write_kv)


def _inproj(x, g, w_bf, *, tm, keep_rows):
    B, L, _ = x.shape
    T = L // tm
    tail = keep_rows // tm
    assert T * tm == L and tail * tm == keep_rows and tail >= 1
    row = lambda w: pl.BlockSpec((None, tm, w), lambda b, t: (b, t, 0))
    kern = functools.partial(_inproj_kernel, tiles_per_batch=T, tail_tiles=tail)
    return pl.pallas_call(
        kern,
        grid=(B, T),
        in_specs=[row(D_MODEL), _const_spec((1, D_MODEL)), _const_spec((D_MODEL, D_IN))],
        out_specs=[row(D_A), row(D_A), row(D_A), row(4 * D_B), row(2 * D_MODEL),
                   pl.BlockSpec((None, tm, 2 * D_A), lambda b, t: (b, jnp.maximum(t - (T - tail), 0), 0))],
        out_shape=[jax.ShapeDtypeStruct((B, L, D_A), BF16)] * 3
        + [jax.ShapeDtypeStruct((B, L, 4 * D_B), F32), jax.ShapeDtypeStruct((B, L, 2 * D_MODEL), BF16),
           jax.ShapeDtypeStruct((B, keep_rows, 2 * D_A), F32)],
        compiler_params=pltpu.CompilerParams(dimension_semantics=("arbitrary", "arbitrary"),
                                             vmem_limit_bytes=VMEM_LIMIT),
        name="inproj",
    )(x, g.reshape(1, D_MODEL), w_bf)


def _build_bias(relb_ref, bias_scr):
    col = lax.broadcasted_iota(jnp.int32, (N_HEADS_A, WIN), 1)
    tab = relb_ref[...] * LOG2E
    base = jnp.broadcast_to(tab[:, N_REL - 1:N_REL], (N_HEADS_A, WIN))
    for t in range(N_REL - 1):
        base = jnp.where(col == (WIN - 1 - t), tab[:, t:t + 1], base)
    row = lax.broadcasted_iota(jnp.int32, (CHUNK, WIN), 0)
    colq = lax.broadcasted_iota(jnp.int32, (CHUNK, WIN), 1)
    for h in range(N_HEADS_A):
        x = jnp.broadcast_to(base[h:h + 1, :], (CHUNK, WIN))
        shift = 1
        while shift < CHUNK:
            x = jnp.where((row & shift) != 0, pltpu.roll(x, shift, 1), x)
            shift *= 2
        bias_scr[h, 0:CHUNK, :] = jnp.where(colq < WIN - CHUNK, pltpu.roll(x, WIN - CHUNK, 1), NEG)
        bias_scr[h, CHUNK:2 * CHUNK, :] = jnp.where(colq >= CHUNK, x, NEG)


def _attend_pair(q2, kw, vw, bias_scr, mrow):
    lane_head = lax.broadcasted_iota(jnp.int32, (PAIR, GROUP_LANES), 1) // HEAD_DIM_A
    outs = []
    for g in range(N_HEADS_A // HEADS_PER_GROUP):
        ls = slice(g * GROUP_LANES, (g + 1) * GROUP_LANES)
        q256 = q2[:, ls]
        qs = jnp.concatenate(
            [jnp.where(lane_head == hh, q256, jnp.zeros_like(q256)) for hh in range(HEADS_PER_GROUP)], axis=0)
        s = _dot_nt(qs, kw[:, ls])
        bias = jnp.concatenate([bias_scr[g * HEADS_PER_GROUP + hh] for hh in range(HEADS_PER_GROUP)], axis=0)
        s = s + bias + mrow
        m = jnp.max(s, axis=-1, keepdims=True)
        p = jnp.exp2(s - m)
        l = jnp.sum(p, axis=-1, keepdims=True)
        pv = _dot(p.astype(BF16), vw[:, ls]) * pl.reciprocal(l, approx=True)
        o = jnp.zeros((PAIR, GROUP_LANES), F32)
        for hh in range(HEADS_PER_GROUP):
            o = jnp.where(lane_head == hh, pv[hh * PAIR:(hh + 1) * PAIR, :], o)
        outs.append(o)
    return jnp.concatenate(outs, axis=1)


def _attn_prompt_kernel(relb_ref, q_ref, k_ref, v_ref, o_ref, kbuf, vbuf, bias_scr, *, qb):
    i = pl.program_id(1)
    hist = ATT_WINDOW

    @pl.when((pl.program_id(0) == 0) & (i == 0))
    def _():
        _build_bias(relb_ref, bias_scr)

    @pl.when(i == 0)
    def _():
        kbuf[0:hist, :] = jnp.zeros((hist, D_A), BF16)
        vbuf[0:hist, :] = jnp.zeros((hist, D_A), BF16)

    @pl.when(i > 0)
    def _():
        kbuf[0:hist, :] = kbuf[qb:qb + hist, :]
        vbuf[0:hist, :] = vbuf[qb:qb + hist, :]

    kbuf[hist:hist + qb, :] = k_ref[...]
    vbuf[hist:hist + qb, :] = v_ref[...]
    col = lax.broadcasted_iota(jnp.int32, (1, WIN), 1)

    for r0 in range(0, qb, PAIR):
        lo = jnp.where(i == 0, hist - r0, 0)
        mrow = jnp.where(col >= lo, 0.0, NEG).astype(F32)
        o = _attend_pair(q_ref[r0:r0 + PAIR, :], kbuf[r0:r0 + WIN, :], vbuf[r0:r0 + WIN, :], bias_scr, mrow)
        o_ref[r0:r0 + PAIR, :] = o.astype(BF16)


def _attn_prompt(relb, q, k, v, *, qb):
    B, L, _ = q.shape
    blk = pl.BlockSpec((None, qb, D_A), lambda b, i: (b, i, 0))
    return pl.pallas_call(
        functools.partial(_attn_prompt_kernel, qb=qb),
        grid=(B, L // qb),
        in_specs=[_const_spec((N_HEADS_A, N_REL)), blk, blk, blk],
        out_specs=blk,
        out_shape=jax.ShapeDtypeStruct((B, L, D_A), BF16),
        scratch_shapes=[pltpu.VMEM((ATT_WINDOW + qb, D_A), BF16), pltpu.VMEM((ATT_WINDOW + qb, D_A), BF16),
                        pltpu.VMEM((N_HEADS_A, PAIR, WIN), F32)],
        compiler_params=pltpu.CompilerParams(dimension_semantics=("arbitrary", "arbitrary"),
                                             vmem_limit_bytes=VMEM_LIMIT),
        name="attn_prompt",
    )(relb, q, k, v)


def _attn_window_kernel(relb_ref, q_ref, k_ref, v_ref, o_ref, bias_scr, *, n_valid):
    @pl.when(pl.program_id(0) == 0)
    def _():
        _build_bias(relb_ref, bias_scr)

    col = lax.broadcasted_iota(jnp.int32, (1, WIN), 1)
    mrow = jnp.where(col < n_valid, 0.0, NEG).astype(F32)
    q2 = jnp.concatenate([q_ref[...], jnp.zeros((CHUNK, D_A), BF16)], axis=0)
    o_ref[...] = _attend_pair(q2, k_ref[...], v_ref[...], bias_scr, mrow)[0:CHUNK, :].astype(BF16)


def _attn_window(relb, qwin, kwin, vwin, *, n_valid):
    B = qwin.shape[0]
    return pl.pallas_call(
        functools.partial(_attn_window_kernel, n_valid=n_valid),
        grid=(B,),
        in_specs=[_const_spec((N_HEADS_A, N_REL)),
                  pl.BlockSpec((None, CHUNK, D_A), lambda b: (b, 0, 0)),
                  pl.BlockSpec((None, WIN, D_A), lambda b: (b, 0, 0)),
                  pl.BlockSpec((None, WIN, D_A), lambda b: (b, 0, 0))],
        out_specs=pl.BlockSpec((None, CHUNK, D_A), lambda b: (b, 0, 0)),
        out_shape=jax.ShapeDtypeStruct((B, CHUNK, D_A), BF16),
        scratch_shapes=[pltpu.VMEM((N_HEADS_A, PAIR, WIN), F32)],
        compiler_params=pltpu.CompilerParams(dimension_semantics=("arbitrary",), vmem_limit_bytes=VMEM_LIMIT),
        name="attn_window",
    )(relb, qwin, kwin, vwin)


def _block_bcast(x, period, src):
    R, C = x.shape
    if period % 8 == 0:
        x3 = x.reshape(R // period, period, C)
        return jnp.broadcast_to(x3[:, src:src + 1, :], x3.shape).reshape(R, C)
    off = lax.broadcasted_iota(jnp.int32, (R, C), 0) % period
    out = x
    for j in range(period):
        if j != src:
            out = jnp.where(off == j, pltpu.roll(x, (j - src) % R, 0), out)
    return out


def _hgrn_kernel(*refs, tb, chunk, has_s0):
    if has_s0:
        hg_ref, lbl_ref, gn_ref, s0_ref, ob_ref, sout_ref, st_scr = refs
    else:
        hg_ref, lbl_ref, gn_ref, ob_ref, sout_ref, st_scr = refs
    i = pl.program_id(1)

    @pl.when(i == 0)
    def _():
        for h in range(N_HEADS_B):
            if has_s0:
                st_scr[h] = s0_ref[h].T
            else:
                st_scr[h] = jnp.zeros((HEAD_V_B, HEAD_K_B), F32)

    lg = lbl_ref[...]
    e = jnp.exp(lg - jnp.max(lg, axis=0, keepdims=True))
    lb = e[0:1, :] / jnp.sum(e, axis=0, keepdims=True)

    levels = []
    m = chunk // 2
    while m >= 1:
        levels.append(m)
        m //= 2
    rc = lax.broadcasted_iota(jnp.int32, (chunk, D_B), 0)
    lefts = [(rc % (2 * m)) < m for m in levels]
    signs = [jnp.where(left, LOG2E, -LOG2E).astype(F32) for left in lefts]
    t_id = lax.broadcasted_iota(jnp.int32, (2 * chunk, chunk), 0) % chunk
    s_id = lax.broadcasted_iota(jnp.int32, (2 * chunk, chunk), 1)
    diff = t_id ^ s_id
    on_diag = diff == 0
    lev_masks = [jnp.where(t_id > s_id, diff, 0) // m == 1 for m in levels]
    first_head = lax.broadcasted_iota(jnp.int32, (chunk, 2 * HEAD_K_B), 1) < HEAD_K_B
    gn = gn_ref[...]

    def pair_nt(x, w):
        zero = jnp.zeros_like(x)
        xs = jnp.concatenate([jnp.where(first_head, x, zero), jnp.where(first_head, zero, x)], axis=0)
        return _dot_nt(xs, w)

    for c in range(tb // chunk):
        rows = slice(c * chunk, (c + 1) * chunk)
        qb = hg_ref[rows, 0:D_B]
        f = lb + (1.0 - lb) * _sigmoid(hg_ref[rows, D_B:2 * D_B])
        q = qb * _sigmoid_gate(qb)
        k = 1.0 - f
        b = jnp.log(f)
        s = 1
        while s < chunk:
            b = b + jnp.where(rc >= s, pltpu.roll(b, s, 0), 0.0)
            s *= 2
        b_end = b[chunk - 1:chunk, :]
        q_bf = q.astype(BF16)
        k_bf = k.astype(BF16)
        qi = (q * jnp.exp(b)).astype(BF16)
        ks = (k * jnp.exp(b_end - b)).astype(BF16)
        v = hg_ref[rows, 2 * D_B:3 * D_B].astype(BF16)
        dec = jnp.exp(b_end)
        zs = []
        for li, m in enumerate(levels):
            edge = _block_bcast(b, 2 * m, m - 1)
            w = jnp.exp2((edge - b) * signs[li])
            zs.append((jnp.where(lefts[li], k, q) * w).astype(BF16))
        for hp in range(N_HEADS_B // 2):
            lp = slice(2 * hp * HEAD_K_B, (2 * hp + 2) * HEAD_K_B)
            a2 = jnp.where(on_diag, pair_nt(q_bf[:, lp], k_bf[:, lp]), 0.0)
            for li in range(len(levels)):
                a2 = jnp.where(lev_masks[li], pair_nt(zs[li][:, lp], zs[li][:, lp]), a2)
            a2 = a2.astype(BF16)
            for h in (2 * hp, 2 * hp + 1):
                ls = slice(h * HEAD_K_B, (h + 1) * HEAD_K_B)
                a = a2[(h - 2 * hp) * chunk:(h - 2 * hp + 1) * chunk, :]
                st = st_scr[h]
                o = _dot_nt(qi[:, ls], st.astype(BF16)) + _dot(a, v[:, ls])
                st_scr[h] = st * dec[:, ls] + _dot_tn(v[:, ls], ks[:, ls])
                on = _rms(o, gn)
                gb = hg_ref[rows, 3 * D_B + h * HEAD_V_B:3 * D_B + (h + 1) * HEAD_V_B]
                ob_ref[rows, ls] = (on * (gb * _sigmoid_gate(gb))).astype(BF16)

    @pl.when(i == pl.num_programs(1) - 1)
    def _():
        for h in range(N_HEADS_B):
            sout_ref[h] = st_scr[h].T


def _hgrn(hg, lb_logits, g_norm, s0, *, tb, chunk):
    B, L, _ = hg.shape
    has_s0 = s0 is not None
    st_spec = pl.BlockSpec((None, N_HEADS_B, HEAD_K_B, HEAD_V_B), lambda b, i: (b, 0, 0, 0))
    in_specs = [pl.BlockSpec((None, tb, 4 * D_B), lambda b, i: (b, i, 0)),
                _const_spec(lb_logits.shape), _const_spec((1, HEAD_V_B))]
    args = [hg, lb_logits, g_norm.reshape(1, HEAD_V_B)]
    if has_s0:
        in_specs.append(st_spec)
        args.append(s0)
    return pl.pallas_call(
        functools.partial(_hgrn_kernel, tb=tb, chunk=chunk, has_s0=has_s0),
        grid=(B, L // tb),
        in_specs=in_specs,
        out_specs=[pl.BlockSpec((None, tb, D_B), lambda b, i: (b, i, 0)), st_spec],
        out_shape=[jax.ShapeDtypeStruct((B, L, D_B), BF16),
                   jax.ShapeDtypeStruct((B, N_HEADS_B, HEAD_K_B, HEAD_V_B), F32)],
        scratch_shapes=[pltpu.VMEM((N_HEADS_B, HEAD_V_B, HEAD_K_B), F32)],
        compiler_params=pltpu.CompilerParams(dimension_semantics=("arbitrary", "arbitrary"),
                                             vmem_limit_bytes=VMEM_LIMIT),
        name="hgrn",
    )(*args)


def _gelu_tanh(x):
    return 0.5 * x * (1.0 + jnp.tanh(0.7978845608028654 * (x + 0.044715 * (x * x * x))))


def _mlp_front(x, oa, ob, zab, wa_ref, wb_ref, wo_ref, g2_ref, wg_ref, wu_ref):
    ma = _dot(oa, wa_ref[...])
    mb = _dot(ob, wb_ref[...])
    za = zab[:, 0:D_MODEL].astype(F32)
    zb = zab[:, D_MODEL:2 * D_MODEL].astype(F32)
    merged = _sigmoid_gate(za) * ma + _sigmoid_gate(zb) * mb
    x1 = x + _dot(merged.astype(BF16), wo_ref[...])
    h2 = _rms(x1, g2_ref[...]).astype(BF16)
    return x1, _dot(h2, wg_ref[...]), _dot(h2, wu_ref[...])


def _mlp_back(x1, ac, u, wd_ref, gf_ref):
    g = (_gelu_tanh(ac) * u).astype(BF16)
    return _rms(x1 + _dot(g, wd_ref[...]), gf_ref[...])


def _mlp_kernel(*refs, tm, sub, seg, has_state):
    if has_state:
        (x_ref, oa_ref, ob_ref, zab_ref, cst_ref, wa_ref, wb_ref, wo_ref, g2_ref, wg_ref, wu_ref, cw_ref, cb_ref,
         wd_ref, gf_ref, y_ref, cnew_ref, a_scr) = refs
    else:
        (x_ref, oa_ref, ob_ref, zab_ref, wa_ref, wb_ref, wo_ref, g2_ref, wg_ref, wu_ref, cw_ref, cb_ref,
         wd_ref, gf_ref, y_ref, cnew_ref, a_scr) = refs
    cw0, cw1, cw2 = cw_ref[0:1, :], cw_ref[1:2, :], cw_ref[2:3, :]
    cb = cb_ref[...]
    if not has_state:
        t = pl.program_id(1)

        @pl.when(t == 0)
        def _():
            a_scr[0:8, :] = jnp.zeros((8, D_FF), F32)

    def conv(slab, base, a_rows):
        n = a_rows.shape[0]
        slab[base + 8:base + 8 + n, :] = a_rows
        return cb + cw0 * slab[base + 6:base + 6 + n, :] + cw1 * slab[base + 7:base + 7 + n, :] + cw2 * a_rows

    for j in range(tm // sub):
        rows = slice(j * sub, (j + 1) * sub)
        x1, a, u = _mlp_front(x_ref[rows, :], oa_ref[rows, :], ob_ref[rows, :], zab_ref[rows, :],
                              wa_ref, wb_ref, wo_ref, g2_ref, wg_ref, wu_ref)
        if has_state:
            parts = []
            for s in range(j * sub // seg, (j + 1) * sub // seg):
                slab = a_scr.at[s]
                slab[6:8, :] = cst_ref[s]
                parts.append(conv(slab, 0, a[s * seg - j * sub:(s + 1) * seg - j * sub, :]))
                cnew_ref[s] = slab[6 + seg:8 + seg, :]
            ac = jnp.concatenate(parts, axis=0)
        else:
            ac = conv(a_scr, j * sub, a)
        y_ref[rows, :] = _mlp_back(x1, ac, u, wd_ref, gf_ref)

    if not has_state:
        a_scr[0:8, :] = a_scr[tm:tm + 8, :]

        @pl.when(t == pl.num_programs(1) - 1)
        def _():
            cnew_ref[...] = a_scr[6:8, :]


def _mlp(x, oa, ob, zab, conv_state, wa, wb, wo, g2, wg, wu, cw, cb, wd, gf, *, tm, sub):
    B, L, _ = x.shape
    has_state = conv_state is not None
    row = lambda w: pl.BlockSpec((None, tm, w), lambda b, t: (b, t, 0))
    in_specs = [row(D_MODEL), row(D_A), row(D_B), row(2 * D_MODEL)]
    args = [x, oa, ob, zab]
    if has_state:
        nseg = conv_state.shape[0]
        seg = L // nseg
        assert tm == L and B == 1
        in_specs.append(_const_spec(conv_state.shape))
        args.append(conv_state)
        cnew_shape = conv_state.shape
        cnew_spec = pl.BlockSpec(cnew_shape, lambda b, t: (0, 0, 0))
        scratch = pltpu.VMEM((nseg, seg + 8, D_FF), F32)
    else:
        seg = tm
        cnew_shape = (B, CONV_W - 1, D_FF)
        cnew_spec = pl.BlockSpec((None, CONV_W - 1, D_FF), lambda b, t: (b, 0, 0))
        scratch = pltpu.VMEM((tm + 8, D_FF), F32)
    weights = [wa, wb, wo, g2.reshape(1, D_MODEL), wg, wu, cw, cb.reshape(1, D_FF), wd, gf.reshape(1, D_MODEL)]
    in_specs += [_const_spec(w.shape) for w in weights]
    return pl.pallas_call(
        functools.partial(_mlp_kernel, tm=tm, sub=sub, seg=seg, has_state=has_state),
        grid=(B, L // tm),
        in_specs=in_specs,
        out_specs=[row(D_MODEL), cnew_spec],
        out_shape=[jax.ShapeDtypeStruct((B, L, D_MODEL), F32), jax.ShapeDtypeStruct(cnew_shape, F32)],
        scratch_shapes=[scratch],
        compiler_params=pltpu.CompilerParams(dimension_semantics=("arbitrary", "arbitrary"),
                                             vmem_limit_bytes=VMEM_LIMIT),
        name="mlp",
    )(*args, *weights)


def _heads(t, n_heads):
    B, L, D = t.shape
    return t.reshape(B, L, n_heads, D // n_heads).transpose(0, 2, 1, 3)


def kernel(x_prompt, x_sample, cache_attn_k, cache_attn_v, state_hgrn, state_ffn_conv, norm_mix_g, w_in, rel_bias,
           hgrn_lb_logits, hgrn_norm_g, w_branch_a, w_branch_b, w_out, norm_ffn_g, w_ffn_gate, w_ffn_up,
           ffn_conv_w, ffn_conv_b, w_ffn_down, norm_final_g):
    depth = w_in.shape[0]
    assert depth == 1, "single-layer step"
    Bp, Lp, _ = x_prompt.shape
    Bs, Ls, _ = x_sample.shape
    w_in_bf = w_in[0].astype(BF16)
    wa, wb, wo = w_branch_a[0].astype(BF16), w_branch_b[0].astype(BF16), w_out[0].astype(BF16)
    wg, wu, wd = w_ffn_gate[0].astype(BF16), w_ffn_up[0].astype(BF16), w_ffn_down[0].astype(BF16)
    mlp_w = (wa, wb, wo, norm_ffn_g[0], wg, wu, ffn_conv_w[0], ffn_conv_b[0], wd, norm_final_g)

    keep = min(ATT_WINDOW, Lp)
    q, k, v, hg, zab, kvf = _inproj(x_prompt, norm_mix_g[0], w_in_bf, tm=512, keep_rows=keep)
    oa = _attn_prompt(rel_bias[0], q, k, v, qb=1024)
    ob, s_p = _hgrn(hg, hgrn_lb_logits, hgrn_norm_g[0], None, tb=512, chunk=CHUNK)
    y_p, conv_p = _mlp(x_prompt, oa, ob, zab, None, *mlp_w, tm=512, sub=256)
    k_p = _heads(kvf[:, :, :D_A], N_HEADS_A)
    v_p = _heads(kvf[:, :, D_A:], N_HEADS_A)

    n_s = Bs * Ls
    xs = x_sample.reshape(1, n_s, D_MODEL)
    q, k, v, hg, zab, kvf = _inproj(xs, norm_mix_g[0], w_in_bf, tm=n_s, keep_rows=n_s)
    w_att = cache_attn_k.shape[3]
    assert w_att == ATT_WINDOW and Ls <= CHUNK
    pad_q = jnp.zeros((Bs, CHUNK - Ls, D_A), BF16)

    def window(cache, new):
        past = cache[0].astype(BF16).transpose(0, 2, 1, 3).reshape(Bs, w_att, D_A)
        return jnp.concatenate([past, new.reshape(Bs, Ls, D_A), pad_q, jnp.zeros((Bs, CHUNK, D_A), BF16)], axis=1)

    qwin = jnp.concatenate([q.reshape(Bs, Ls, D_A), pad_q], axis=1)
    oa = _attn_window(rel_bias[0], qwin, window(cache_attn_k, k), window(cache_attn_v, v),
                      n_valid=w_att + Ls)[:, :Ls].reshape(1, n_s, D_A)
    ob, s_s = _hgrn(hg.reshape(Bs, Ls, 4 * D_B), hgrn_lb_logits, hgrn_norm_g[0], state_hgrn[0], tb=Ls, chunk=Ls)
    y_s, conv_s = _mlp(xs, oa, ob.reshape(1, n_s, D_B), zab, state_ffn_conv[0], *mlp_w, tm=n_s, sub=n_s)
    kvf = kvf.reshape(Bs, Ls, 2 * D_A)
    k_s = _heads(kvf[:, :, :D_A], N_HEADS_A)
    v_s = _heads(kvf[:, :, D_A:], N_HEADS_A)

    return (y_p, y_s.reshape(Bs, Ls, D_MODEL),
            k_p[None], v_p[None], s_p[None], conv_p[None],
            k_s[None], v_s[None], s_s[None], conv_s[None])
```
